```python
import jax, jax.numpy as jnp
from jax import lax
import numpy as np

D_MODEL = 1024
BATCH = 32
SEQ = 2048
DEPTH = 4
DEC_BATCH = 32
DEC_SEQ = 32
PAST_LEN = 2048

CHUNK = 64
N_MIXERS = 2
SB_HEADS = 16
SB_HEAD_DIM = D_MODEL // SB_HEADS
Q_BLOCK = 128
D_RNN = (4 * D_MODEL // 3) // 128 * 128
LRU_BLOCKS = 16
LRU_BLOCK = D_RNN // LRU_BLOCKS
CONV_W = 4
LRU_C = 8.0
D_FF = ((8 * D_MODEL // 3 + 255) // 256) * 256
PLE_DIM = 256
EPS = 1e-6

kernel_name = "stickbreak_rglru_macaron_stream_step"


def rmsnorm(x, g):
    xf = x.astype(jnp.float32)
    y = xf * lax.rsqrt(jnp.mean(xf * xf, axis=-1, keepdims=True) + EPS)
    return (y * g.astype(jnp.float32)).astype(x.dtype)


def swiglu(x, w_gate, w_up, w_down):
    return (jax.nn.silu(x @ w_gate) * (x @ w_up)) @ w_down


def sb_attend(q, k, v, q_pos, k_pos):
    z = jnp.einsum('bqhd,bkhd->bhqk', q.astype(jnp.float32), k.astype(jnp.float32)) * (SB_HEAD_DIM ** -0.5)
    mask = k_pos[None, :] < q_pos[:, None]
    log_stay = jnp.where(mask, jax.nn.log_sigmoid(-z), 0.0)
    after = lax.cumsum(log_stay, axis=3, reverse=True) - log_stay
    w = jnp.where(mask, jnp.exp(jax.nn.log_sigmoid(z) + after), 0.0)
    return jnp.einsum('bhqk,bkhd->bqhd', w, v.astype(jnp.float32)).astype(q.dtype)


def sb_qkv(xn, w_qkv):
    B, T, _ = xn.shape
    qkv = (xn @ w_qkv).reshape(B, T, 3, SB_HEADS, SB_HEAD_DIM)
    return qkv[:, :, 0], qkv[:, :, 1], qkv[:, :, 2]


def sb_mixer_prompt(xn, w_qkv, w_o):
    B, T, _ = xn.shape
    q, k, v = sb_qkv(xn, w_qkv)
    pos = jnp.arange(T)
    outs = []
    for b0 in range(0, T, Q_BLOCK):
        b1 = min(b0 + Q_BLOCK, T)
        outs.append(sb_attend(q[:, b0:b1], k[:, :b1], v[:, :b1], pos[b0:b1], pos[:b1]))
    o = jnp.concatenate(outs, axis=1).reshape(B, T, D_MODEL)
    return o @ w_o, k, v


def sb_mixer_sample(xn, cache_k, cache_v, w_qkv, w_o):
    B, T, _ = xn.shape
    P = cache_k.shape[1]
    q, k, v = sb_qkv(xn, w_qkv)
    k_all = jnp.concatenate([cache_k.astype(k.dtype), k], axis=1)
    v_all = jnp.concatenate([cache_v.astype(v.dtype), v], axis=1)
    o = sb_attend(q, k_all, v_all, P + jnp.arange(T), jnp.arange(P + T))
    return o.reshape(B, T, D_MODEL) @ w_o, k, v


def _lin_combine(e1, e2):
    a1, b1 = e1
    a2, b2 = e2
    return a1 * a2, a2 * b1 + b2


def lru_mixer(xn, h0, conv0, w_in, conv_w, conv_b, w_a, b_a, w_x, b_x, lam, w_o):
    B, T, _ = xn.shape
    gx = xn @ w_in
    gate = jax.nn.gelu(gx[..., :D_RNN])
    u = gx[..., D_RNN:]
    ext = jnp.concatenate([conv0.astype(u.dtype), u], axis=1)
    c = conv_b + sum(ext[:, j:j + T] * conv_w[j] for j in range(CONV_W))
    new_conv = ext[:, -(CONV_W - 1):]
    cb = c.reshape(B, T, LRU_BLOCKS, LRU_BLOCK)
    r = jax.nn.sigmoid(jnp.einsum('btnj,njk->btnk', cb, w_a).reshape(B, T, D_RNN) + b_a)
    gin = jax.nn.sigmoid(jnp.einsum('btnj,njk->btnk', cb, w_x).reshape(B, T, D_RNN) + b_x)
    log_a = (-LRU_C * jax.nn.softplus(-lam.astype(jnp.float32))) * r.astype(jnp.float32)
    a = jnp.exp(log_a)
    bterm = jnp.sqrt(-jnp.expm1(2.0 * log_a)) * (gin * c).astype(jnp.float32)
    bterm = bterm.at[:, 0].add(a[:, 0] * h0.astype(jnp.float32))
    _, h = lax.associative_scan(_lin_combine, (a, bterm), axis=1)
    y = (h.astype(xn.dtype) * gate) @ w_o
    return y, h[:, -1], new_conv


def layer_forward(x, p_i, norm_g, w1g, w1u, w1d, w2g, w2u, w2d, w_pe, w_pg, mix):
    x = x + 0.5 * swiglu(rmsnorm(x, norm_g[0]), w1g, w1u, w1d)
    y, st = mix(rmsnorm(x, norm_g[1]))
    x = x + y
    x = x + 0.5 * swiglu(rmsnorm(x, norm_g[2]), w2g, w2u, w2d)
    x = x + (p_i @ w_pe) * jax.nn.sigmoid(rmsnorm(x, norm_g[3]) @ w_pg)
    return x, st


def setup_inputs(seed: int = 0) -> dict:
    key = jax.random.key(seed)
    ks = jax.random.split(key, 32)
    n_sb = (DEPTH + 1) // 2
    n_lru = DEPTH // 2
    f32 = jnp.float32

    def nrm(k, shape, scale):
        return jax.random.normal(k, shape, f32) * scale

    u = jax.random.uniform(ks[25], (n_lru, D_RNN), f32, 0.9, 0.999)
    a_base = u ** (1.0 / LRU_C)
    lam = jnp.log(a_base) - jnp.log1p(-a_base)
    return {
        'x_prompt': nrm(ks[0], (BATCH, SEQ, D_MODEL), 1.0),
        'x_sample': nrm(ks[1], (DEC_BATCH, DEC_SEQ, D_MODEL), 1.0),
        'p_prompt': nrm(ks[2], (DEPTH, BATCH, SEQ, PLE_DIM), 1.0),
        'p_sample': nrm(ks[3], (DEPTH, DEC_BATCH, DEC_SEQ, PLE_DIM), 1.0),
        'cache_sb_k': nrm(ks[4], (n_sb, DEC_BATCH, PAST_LEN, SB_HEADS, SB_HEAD_DIM), 1.0),
        'cache_sb_v': nrm(ks[5], (n_sb, DEC_BATCH, PAST_LEN, SB_HEADS, SB_HEAD_DIM), 1.0),
        'state_lru_h': nrm(ks[6], (n_lru, DEC_BATCH, D_RNN), 0.5),
        'state_lru_conv': nrm(ks[7], (n_lru, DEC_BATCH, CONV_W - 1, D_RNN), 1.0),
        'norm_g': 1.0 + nrm(ks[8], (DEPTH, 4, D_MODEL), 0.02),
        'ffn_w_gate': nrm(ks[9], (DEPTH, 2, D_MODEL, D_FF), D_MODEL ** -0.5),
        'ffn_w_up': nrm(ks[10], (DEPTH, 2, D_MODEL, D_FF), D_MODEL ** -0.5),
        'ffn_w_down': nrm(ks[11], (DEPTH, 2, D_FF, D_MODEL), D_FF ** -0.5),
        'ple_w_proj': nrm(ks[12], (DEPTH, PLE_DIM, D_MODEL), PLE_DIM ** -0.5),
        'ple_w_gate': nrm(ks[13], (DEPTH, D_MODEL, D_MODEL), D_MODEL ** -0.5),
        'sb_w_qkv': nrm(ks[14], (n_sb, D_MODEL, 3 * D_MODEL), D_MODEL ** -0.5),
        'sb_w_o': nrm(ks[15], (n_sb, D_MODEL, D_MODEL), D_MODEL ** -0.5),
        'lru_w_in': nrm(ks[16], (n_lru, D_MODEL, 2 * D_RNN), D_MODEL ** -0.5),
        'lru_conv_w': nrm(ks[17], (n_lru, CONV_W, D_RNN), CONV_W ** -0.5),
        'lru_conv_b': nrm(ks[18], (n_lru, D_RNN), 0.01),
        'lru_w_a': nrm(ks[19], (n_lru, LRU_BLOCKS, LRU_BLOCK, LRU_BLOCK), LRU_BLOCK ** -0.5),
        'lru_b_a': nrm(ks[20], (n_lru, D_RNN), 0.01),
        'lru_w_x': nrm(ks[21], (n_lru, LRU_BLOCKS, LRU_BLOCK, LRU_BLOCK), LRU_BLOCK ** -0.5),
        'lru_b_x': nrm(ks[22], (n_lru, D_RNN), 0.01),
        'lru_lambda': lam,
        'lru_w_o': nrm(ks[23], (n_lru, D_RNN, D_MODEL), D_RNN ** -0.5),
        'final_norm_g': 1.0 + nrm(ks[24], (D_MODEL,), 0.02),
    }


def reference(x_prompt, x_sample, p_prompt, p_sample, cache_sb_k, cache_sb_v, state_lru_h, state_lru_conv,
              norm_g, ffn_w_gate, ffn_w_up, ffn_w_down, ple_w_proj, ple_w_gate, sb_w_qkv, sb_w_o,
              lru_w_in, lru_conv_w, lru_conv_b, lru_w_a, lru_b_a, lru_w_x, lru_b_x, lru_lambda, lru_w_o,
              final_norm_g):
    xp, xs = x_prompt, x_sample
    kp, vp, ks_, vs_ = [], [], [], []
    hp, cp, hs, cs = [], [], [], []
    for i in range(DEPTH):
        j = i // N_MIXERS
        if i % N_MIXERS == 0:
            wq, wo = sb_w_qkv[j], sb_w_o[j]
            ck, cv = cache_sb_k[j], cache_sb_v[j]
            mix_p = lambda xn, wq=wq, wo=wo: (lambda r: (r[0], (r[1], r[2])))(sb_mixer_prompt(xn, wq, wo))
            mix_s = lambda xn, wq=wq, wo=wo, ck=ck, cv=cv: (lambda r: (r[0], (r[1], r[2])))(sb_mixer_sample(xn, ck, cv, wq, wo))
        else:
            lw = (lru_w_in[j], lru_conv_w[j], lru_conv_b[j], lru_w_a[j], lru_b_a[j],
                  lru_w_x[j], lru_b_x[j], lru_lambda[j], lru_w_o[j])
            h0s, c0s = state_lru_h[j], state_lru_conv[j]
            mix_p = lambda xn, lw=lw: (lambda r: (r[0], (r[1], r[2])))(lru_mixer(
                xn, jnp.zeros((xn.shape[0], D_RNN), jnp.float32),
                jnp.zeros((xn.shape[0], CONV_W - 1, D_RNN), xn.dtype), *lw))
            mix_s = lambda xn, lw=lw, h0s=h0s, c0s=c0s: (lambda r: (r[0], (r[1], r[2])))(lru_mixer(xn, h0s, c0s, *lw))
        common = (norm_g[i], ffn_w_gate[i, 0], ffn_w_up[i, 0], ffn_w_down[i, 0],
                  ffn_w_gate[i, 1], ffn_w_up[i, 1], ffn_w_down[i, 1], ple_w_proj[i], ple_w_gate[i])
        xp, stp = layer_forward(xp, p_prompt[i], *common, mix_p)
        xs, sts = layer_forward(xs, p_sample[i], *common, mix_s)
        if i % N_MIXERS == 0:
            kp.append(stp[0]); vp.append(stp[1]); ks_.append(sts[0]); vs_.append(sts[1])
        else:
            hp.append(stp[0]); cp.append(stp[1]); hs.append(sts[0]); cs.append(sts[1])
    y_prompt = rmsnorm(xp, final_norm_g)
    y_sample = rmsnorm(xs, final_norm_g)
    return (y_prompt, y_sample,
            jnp.stack(kp), jnp.stack(vp), jnp.stack(hp), jnp.stack(cp),
            jnp.stack(ks_), jnp.stack(vs_), jnp.stack(hs), jnp.stack(cs))
```

```python
import functools
import math

import jax
import jax.numpy as jnp
from jax import lax
from jax.experimental import pallas as pl
from jax.experimental.pallas import tpu as pltpu

D_MODEL = 1024
SB_HEADS = 16
SB_HEAD_DIM = 64
D_RNN = 1280
LRU_BLOCKS = 16
LRU_BLOCK = 80
LRU_GROUP = 640
CONV_W = 4
LRU_C = 8.0
D_FF = 2816
PLE_DIM = 256
EPS = 1e-6
LOG2E = 1.4426950408889634

BF16 = jnp.bfloat16
F32 = jnp.float32

V7X_VMEM_LIMIT = 56 * 1024 * 1024


def _const_spec(shape):
    nd = len(shape)
    return pl.BlockSpec(shape, lambda *_: (0,) * nd, pipeline_mode=pl.Buffered(1))


def _rms(x, g):
    ms = jnp.mean(x * x, axis=-1, keepdims=True)
    return x * lax.rsqrt(ms + EPS) * g


def _dot(a, b):
    return jnp.dot(a, b, preferred_element_type=F32)


def _sigmoid(x):
    return 1.0 / (1.0 + jnp.exp(-x))


FF_CHUNK = 256


def _swiglu_half(x, g, wg_ref, wu_ref, wd_ref):
    xn = _rms(x, g).astype(BF16)
    acc = jnp.zeros(x.shape, F32)
    for c in range(D_FF // FF_CHUNK):
        sl = slice(c * FF_CHUNK, (c + 1) * FF_CHUNK)
        gate = _dot(xn, wg_ref[:, sl])
        up = _dot(xn, wu_ref[:, sl])
        h = (gate * _sigmoid(gate) * up).astype(BF16)
        acc = acc + _dot(h, wd_ref[sl, :])
    return x + 0.5 * acc


def _ffn1_kernel(x_ref, g_ref, wg_ref, wu_ref, wd_ref, o_ref):
    o_ref[...] = _swiglu_half(x_ref[...], g_ref[...], wg_ref, wu_ref, wd_ref)


def _ffn2_ple_kernel(x_ref, a_ref, wo_ref, p_ref, g2_ref, wg_ref, wu_ref, wd_ref, g3_ref, wpe_ref, wpg_ref,
                     gf_ref, o_ref, *, final_norm):
    x = x_ref[...] + _dot(a_ref[...], wo_ref[...])
    x = _swiglu_half(x, g2_ref[...], wg_ref, wu_ref, wd_ref)
    xn = _rms(x, g3_ref[...]).astype(BF16)
    gate = _sigmoid(_dot(xn, wpg_ref[...]))
    pe = _dot(p_ref[...].astype(BF16), wpe_ref[...])
    x = x + pe * gate
    if final_norm:
        x = _rms(x, gf_ref[...])
    o_ref[...] = x


def _qkv_kernel(x_ref, g_ref, w_ref, q_ref, k_ref, v_ref, kb_ref, vb_ref):
    xn = _rms(x_ref[...], g_ref[...]).astype(BF16)
    d = D_MODEL
    q_ref[...] = (_dot(xn, w_ref[:, 0:d]) * (SB_HEAD_DIM ** -0.5 * LOG2E)).astype(BF16)
    k = _dot(xn, w_ref[:, d:2 * d])
    k_ref[...] = k
    kb_ref[...] = k.astype(BF16)
    v = _dot(xn, w_ref[:, 2 * d:3 * d])
    v_ref[...] = v
    vb_ref[...] = v.astype(BF16)


def _token_tile(n):
    for tm in (512, 256, 128):
        if n % tm == 0:
            return tm
    raise ValueError(f"token count {n} not a multiple of 128")


def _tok_spec(tm, width):
    return pl.BlockSpec((tm, width), lambda i: (i, 0))


def _cparams(sem):
    return pltpu.CompilerParams(dimension_semantics=sem, vmem_limit_bytes=V7X_VMEM_LIMIT)


def ffn1(x, g, wg, wu, wd):
    n = x.shape[0]
    tm = _token_tile(n)
    return pl.pallas_call(
        _ffn1_kernel,
        grid=(n // tm,),
        in_specs=[_tok_spec(tm, D_MODEL), _const_spec((1, D_MODEL)), _const_spec(wg.shape), _const_spec(wu.shape),
                  _const_spec(wd.shape)],
        out_specs=_tok_spec(tm, D_MODEL),
        out_shape=jax.ShapeDtypeStruct(x.shape, F32),
        compiler_params=_cparams(("parallel",)),
        name="ffn1",
    )(x, g, wg, wu, wd)


def ffn2_ple(x, a, wo, p, g2, wg, wu, wd, g3, wpe, wpg, gf, final_norm):
    n = x.shape[0]
    tm = _token_tile(n)
    ka = a.shape[1]
    return pl.pallas_call(
        functools.partial(_ffn2_ple_kernel, final_norm=final_norm),
        grid=(n // tm,),
        in_specs=[_tok_spec(tm, D_MODEL), _tok_spec(tm, ka), _const_spec(wo.shape), _tok_spec(tm, PLE_DIM),
                  _const_spec((1, D_MODEL)), _const_spec(wg.shape), _const_spec(wu.shape), _const_spec(wd.shape),
                  _const_spec((1, D_MODEL)), _const_spec(wpe.shape), _const_spec(wpg.shape),
                  _const_spec((1, D_MODEL))],
        out_specs=_tok_spec(tm, D_MODEL),
        out_shape=jax.ShapeDtypeStruct(x.shape, F32),
        compiler_params=_cparams(("parallel",)),
        name="ffn2_ple",
    )(x, a, wo, p, g2, wg, wu, wd, g3, wpe, wpg, gf)


def qkv_proj(x, g, w):
    n = x.shape[0]
    tm = _token_tile(n)
    tok = _tok_spec(tm, D_MODEL)
    return pl.pallas_call(
        _qkv_kernel,
        grid=(n // tm,),
        in_specs=[tok, _const_spec((1, D_MODEL)), _const_spec(w.shape)],
        out_specs=[tok] * 5,
        out_shape=[jax.ShapeDtypeStruct(x.shape, BF16), jax.ShapeDtypeStruct(x.shape, F32),
                   jax.ShapeDtypeStruct(x.shape, F32), jax.ShapeDtypeStruct(x.shape, BF16),
                   jax.ShapeDtypeStruct(x.shape, BF16)],
        compiler_params=_cparams(("parallel",)),
        name="qkv_proj",
    )(x, g, w)


ATT_TILE = 256


def _upper_ones(n):
    r = lax.broadcasted_iota(jnp.int32, (n, n), 0)
    c = lax.broadcasted_iota(jnp.int32, (n, n), 1)
    return jnp.where(c > r, 1.0, 0.0).astype(BF16)


def _nt_dot(a, b):
    return lax.dot_general(a, b, (((1,), (1,)), ((), ())), preferred_element_type=F32)


def _tn_dot(a, b):
    return lax.dot_general(a, b, (((0,), (0,)), ((), ())), preferred_element_type=F32)


def _sb_tile(kt, vt, qm, upper, carry, visible):
    z = _nt_dot(kt, qm)
    log_stay = -(jnp.maximum(z, 0.0) + jnp.log2(1.0 + jnp.exp2(-jnp.abs(z))))
    if visible is not None:
        log_stay = jnp.where(visible, log_stay, 0.0)
    hi = log_stay.astype(BF16)
    lo = (log_stay - hi.astype(F32)).astype(BF16)
    after = _dot(upper, hi) + _dot(upper, lo)
    w = jnp.exp2(z + log_stay + after + carry)
    if visible is not None:
        w = jnp.where(visible, w, 0.0)
    out_t = _tn_dot(vt, w.astype(BF16))
    new_carry = carry + after[0:1, :] + log_stay[0:1, :]
    return out_t, new_carry


def _sb_prompt_kernel(q_ref, k_ref, v_ref, o_ref, acc_ref, carry_ref):
    t = q_ref.shape[1]
    n_blocks = t // ATT_TILE
    upper = _upper_ones(ATT_TILE)
    lane = lax.broadcasted_iota(jnp.int32, (1, 128), 1)
    head_a = lane < SB_HEAD_DIM
    kpos = lax.broadcasted_iota(jnp.int32, (ATT_TILE, ATT_TILE), 0)
    qpos = lax.broadcasted_iota(jnp.int32, (ATT_TILE, ATT_TILE), 1)
    diag_visible = kpos < qpos

    def q_block(qb, _):
        q0 = pl.multiple_of(qb * ATT_TILE, ATT_TILE)
        q2 = q_ref[0, pl.ds(q0, ATT_TILE), :]
        qa = jnp.where(head_a, q2, jnp.zeros_like(q2))
        qb_ = jnp.where(head_a, jnp.zeros_like(q2), q2)
        kt = k_ref[0, pl.ds(q0, ATT_TILE), :]
        vt = v_ref[0, pl.ds(q0, ATT_TILE), :]
        zero_carry = jnp.zeros((1, ATT_TILE), F32)
        for h, qm in enumerate((qa, qb_)):
            out_t, carry = _sb_tile(kt, vt, qm, upper, zero_carry, diag_visible)
            acc_ref[h] = out_t
            carry_ref[h] = jnp.broadcast_to(carry, (8, ATT_TILE))

        def k_block(i, _):
            k0 = pl.multiple_of((qb - 1 - i) * ATT_TILE, ATT_TILE)
            kt = k_ref[0, pl.ds(k0, ATT_TILE), :]
            vt = v_ref[0, pl.ds(k0, ATT_TILE), :]
            for h, qm in enumerate((qa, qb_)):
                out_t, carry = _sb_tile(kt, vt, qm, upper, carry_ref[h, 0:1, :], None)
                acc_ref[h] += out_t
                carry_ref[h] = jnp.broadcast_to(carry, (8, ATT_TILE))
            return 0

        lax.fori_loop(0, qb, k_block, 0)
        row = lax.broadcasted_iota(jnp.int32, (128, 1), 0)
        out_t = jnp.where(row < SB_HEAD_DIM, acc_ref[0], acc_ref[1])
        o_ref[0, pl.ds(q0, ATT_TILE), :] = out_t.T.astype(BF16)
        return 0

    lax.fori_loop(0, n_blocks, q_block, 0)


def sb_attention_prompt(q, k, v):
    b, t, d = q.shape
    spec = pl.BlockSpec((1, t, 128), lambda i, j: (i, 0, j))
    return pl.pallas_call(
        _sb_prompt_kernel,
        grid=(b, d // 128),
        in_specs=[spec, spec, spec],
        out_specs=spec,
        out_shape=jax.ShapeDtypeStruct(q.shape, BF16),
        scratch_shapes=[pltpu.VMEM((2, 128, ATT_TILE), F32), pltpu.VMEM((2, 8, ATT_TILE), F32)],
        compiler_params=_cparams(("parallel", "parallel")),
        name="sb_attn_prompt",
    )(q, k, v)


SAMPLE_HEADS = 4
SAMPLE_W = SAMPLE_HEADS * SB_HEAD_DIM


def _sb_sample_kernel(q_ref, kn_ref, vn_ref, ck_ref, cv_ref, o_ref, acc_ref, carry_ref):
    tq = q_ref.shape[1]
    past = ck_ref.shape[1]
    lanes = SAMPLE_HEADS * tq
    upper = _upper_ones(ATT_TILE)
    q4 = q_ref[0]
    qrows = jnp.concatenate([q4] * SAMPLE_HEADS, axis=0)
    r_head = lax.broadcasted_iota(jnp.int32, (lanes, SAMPLE_W), 0) // tq
    c_head = lax.broadcasted_iota(jnp.int32, (lanes, SAMPLE_W), 1) // SB_HEAD_DIM
    own = r_head == c_head
    qm = jnp.where(own, qrows, jnp.zeros_like(qrows))

    kpos = lax.broadcasted_iota(jnp.int32, (tq, lanes), 0)
    qpos = lax.broadcasted_iota(jnp.int32, (tq, lanes), 1) % tq
    out_t, carry = _sb_tile(kn_ref[0], vn_ref[0], qm, upper[0:tq, 0:tq], jnp.zeros((1, lanes), F32), kpos < qpos)
    acc_ref[...] = out_t
    carry_ref[...] = jnp.broadcast_to(carry, (8, lanes))

    def k_block(i, _):
        k0 = pl.multiple_of(past - (i + 1) * ATT_TILE, ATT_TILE)
        kt = ck_ref[0, pl.ds(k0, ATT_TILE), :].astype(BF16)
        vt = cv_ref[0, pl.ds(k0, ATT_TILE), :].astype(BF16)
        out_t, carry = _sb_tile(kt, vt, qm, upper, carry_ref[0:1, :], None)
        acc_ref[...] += out_t
        carry_ref[...] = jnp.broadcast_to(carry, (8, lanes))
        return 0

    lax.fori_loop(0, past // ATT_TILE, k_block, 0)
    acc_t = acc_ref[...].T
    acc_t = jnp.where(own, acc_t, 0.0)
    out = acc_t[0:tq]
    for h in range(1, SAMPLE_HEADS):
        out = out + acc_t[h * tq:(h + 1) * tq]
    o_ref[0] = out.astype(BF16)


def sb_attention_sample(q, kn, vn, cache_k, cache_v):
    b, tq, d = q.shape
    past = cache_k.shape[1]
    new_spec = pl.BlockSpec((1, tq, SAMPLE_W), lambda i, j: (i, 0, j))
    cache_spec = pl.BlockSpec((1, past, SAMPLE_W), lambda i, j: (i, 0, j))
    lanes = SAMPLE_HEADS * tq
    return pl.pallas_call(
        _sb_sample_kernel,
        grid=(b, d // SAMPLE_W),
        in_specs=[new_spec, new_spec, new_spec, cache_spec, cache_spec],
        out_specs=new_spec,
        out_shape=jax.ShapeDtypeStruct(q.shape, BF16),
        scratch_shapes=[pltpu.VMEM((SAMPLE_W, lanes), F32), pltpu.VMEM((8, lanes), F32)],
        compiler_params=_cparams(("parallel", "parallel")),
        name="sb_attn_sample",
    )(q, kn, vn, cache_k, cache_v)


TAIL_ROWS = 8


def _gelu_tanh(x):
    return 0.5 * x * (1.0 + jnp.tanh(math.sqrt(2.0 / math.pi) * (x + 0.044715 * (x * x * x))))


def _block_diag_dot(cb, w_ref):
    parts = [_dot(cb[:, g * LRU_GROUP:(g + 1) * LRU_GROUP], w_ref[g]) for g in range(D_RNN // LRU_GROUP)]
    return jnp.concatenate(parts, axis=-1)


def _lru_kernel(x_ref, g_ref, win_ref, cw_ref, cb_ref, wa_ref, ba_ref, wx_ref, bx_ref, lam_ref, h0_ref, c0_ref,
                y_ref, hn_ref, cn_ref, ext_ref, a_ref, b_ref, h_ref, hstate_ref):
    tc = x_ref.shape[1]
    step = pl.program_id(1)

    @pl.when(step == 0)
    def _():
        hstate_ref[...] = h0_ref[0]
        ext_ref[0:TAIL_ROWS - (CONV_W - 1), :] = jnp.zeros((TAIL_ROWS - (CONV_W - 1), D_RNN), F32)
        ext_ref[TAIL_ROWS - (CONV_W - 1):TAIL_ROWS, :] = c0_ref[0]

    xn = _rms(x_ref[0], g_ref[...]).astype(BF16)
    gx = _dot(xn, win_ref[...])
    gate = _gelu_tanh(gx[:, :D_RNN])
    u = gx[:, D_RNN:]
    ext_ref[TAIL_ROWS:, :] = u
    c = cb_ref[...] + u * cw_ref[CONV_W - 1:CONV_W, :]
    for j in range(CONV_W - 1):
        shift = CONV_W - 1 - j
        c = c + ext_ref[TAIL_ROWS - shift:TAIL_ROWS - shift + tc, :] * cw_ref[j:j + 1, :]
    new_tail = ext_ref[tc + TAIL_ROWS - (CONV_W - 1):tc + TAIL_ROWS, :]
    ext_ref[TAIL_ROWS - (CONV_W - 1):TAIL_ROWS, :] = new_tail
    cn_ref[0] = new_tail

    cbf = c.astype(BF16)
    r = _sigmoid(_block_diag_dot(cbf, wa_ref) + ba_ref[...])
    gin = _sigmoid(_block_diag_dot(cbf, wx_ref) + bx_ref[...])
    lam = lam_ref[...]
    softplus_neg_lam = jnp.maximum(-lam, 0.0) + jnp.log(1.0 + jnp.exp(-jnp.abs(lam)))
    log_a = (-LRU_C * softplus_neg_lam) * r
    a = jnp.exp(log_a)
    a_ref[...] = a
    b_ref[...] = jnp.sqrt(-jnp.tanh(log_a) * (a * a + 1.0)) * (gin * c)

    def scan_step(i, h):
        h = a_ref[pl.ds(i, 1), :] * h + b_ref[pl.ds(i, 1), :]
        h_ref[pl.ds(i, 1), :] = h
        return h

    h_last = lax.fori_loop(0, tc, scan_step, hstate_ref[...], unroll=8)
    hstate_ref[...] = h_last
    hn_ref[0] = h_last
    y_ref[0] = (h_ref[...] * gate).astype(BF16)


def lru_core(x, g, w_in, conv_w, conv_b, wa_bd, b_a, wx_bd, b_x, lam, h0, conv0):
    b, t, d = x.shape
    tc = 256 if t % 256 == 0 else t
    vec = _const_spec((1, D_RNN))
    return pl.pallas_call(
        _lru_kernel,
        grid=(b, t // tc),
        in_specs=[pl.BlockSpec((1, tc, d), lambda i, j: (i, j, 0)), _const_spec((1, D_MODEL)),
                  _const_spec(w_in.shape), _const_spec((CONV_W, D_RNN)), vec, _const_spec(wa_bd.shape), vec,
                  _const_spec(wx_bd.shape), vec, vec,
                  pl.BlockSpec((1, 1, D_RNN), lambda i, j: (i, 0, 0)),
                  pl.BlockSpec((1, CONV_W - 1, D_RNN), lambda i, j: (i, 0, 0))],
        out_specs=[pl.BlockSpec((1, tc, D_RNN), lambda i, j: (i, j, 0)),
                   pl.BlockSpec((1, 1, D_RNN), lambda i, j: (i, 0, 0)),
                   pl.BlockSpec((1, CONV_W - 1, D_RNN), lambda i, j: (i, 0, 0))],
        out_shape=[jax.ShapeDtypeStruct((b, t, D_RNN), BF16), jax.ShapeDtypeStruct((b, 1, D_RNN), F32),
                   jax.ShapeDtypeStruct((b, CONV_W - 1, D_RNN), F32)],
        scratch_shapes=[pltpu.VMEM((tc + TAIL_ROWS, D_RNN), F32), pltpu.VMEM((tc, D_RNN), F32),
                        pltpu.VMEM((tc, D_RNN), F32), pltpu.VMEM((tc, D_RNN), F32), pltpu.VMEM((1, D_RNN), F32)],
        compiler_params=_cparams(("parallel", "arbitrary")),
        name="lru_core",
    )(x, g, w_in, conv_w, conv_b, wa_bd, b_a, wx_bd, b_x, lam, h0, conv0)


def _block_diag_groups(w):
    per = LRU_GROUP // LRU_BLOCK
    wg = w.reshape(D_RNN // LRU_GROUP, per, LRU_BLOCK, LRU_BLOCK)
    eye = jnp.eye(per, dtype=w.dtype)
    bd = jnp.einsum('gnjk,nm->gnjmk', wg, eye)
    return bd.reshape(D_RNN // LRU_GROUP, LRU_GROUP, LRU_GROUP).astype(BF16)


def kernel(x_prompt, x_sample, p_prompt, p_sample, cache_sb_k, cache_sb_v, state_lru_h, state_lru_conv, norm_g,
           ffn_w_gate, ffn_w_up, ffn_w_down, ple_w_proj, ple_w_gate, sb_w_qkv, sb_w_o, lru_w_in, lru_conv_w,
           lru_conv_b, lru_w_a, lru_b_a, lru_w_x, lru_b_x, lru_lambda, lru_w_o, final_norm_g):
    depth = norm_g.shape[0]
    bp, tp, d = x_prompt.shape
    bs, ts, _ = x_sample.shape
    past = cache_sb_k.shape[2]
    xs = {'p': x_prompt.reshape(bp * tp, d), 's': x_sample.reshape(bs * ts, d)}
    ps = {'p': p_prompt.reshape(depth, bp * tp, PLE_DIM), 's': p_sample.reshape(depth, bs * ts, PLE_DIM)}
    dims = {'p': (bp, tp), 's': (bs, ts)}
    gf = final_norm_g.reshape(1, d)
    new_k = {'p': [], 's': []}
    new_v = {'p': [], 's': []}
    new_h = {'p': [], 's': []}
    new_c = {'p': [], 's': []}

    for i in range(depth):
        j = i // 2
        g = norm_g[i].reshape(4, 1, d)
        wg1, wu1, wd1 = (w[i, 0].astype(BF16) for w in (ffn_w_gate, ffn_w_up, ffn_w_down))
        wg2, wu2, wd2 = (w[i, 1].astype(BF16) for w in (ffn_w_gate, ffn_w_up, ffn_w_down))
        wpe = ple_w_proj[i].astype(BF16)
        wpg = ple_w_gate[i].astype(BF16)
        if i % 2 == 0:
            wqkv = sb_w_qkv[j].astype(BF16)
            wo = sb_w_o[j].astype(BF16)
        else:
            w_in = lru_w_in[j].astype(BF16)
            wa_bd = _block_diag_groups(lru_w_a[j])
            wx_bd = _block_diag_groups(lru_w_x[j])
            wo = lru_w_o[j].astype(BF16)
            vecs = [v[j].reshape(1, D_RNN) for v in (lru_conv_b, lru_b_a, lru_b_x, lru_lambda)]
        for grp in ('p', 's'):
            b, t = dims[grp]
            x = ffn1(xs[grp], g[0], wg1, wu1, wd1)
            if i % 2 == 0:
                q, k, v, kb, vb = qkv_proj(x, g[1], wqkv)
                new_k[grp].append(k.reshape(b, t, SB_HEADS, SB_HEAD_DIM))
                new_v[grp].append(v.reshape(b, t, SB_HEADS, SB_HEAD_DIM))
                q3, kb3, vb3 = (a.reshape(b, t, d) for a in (q, kb, vb))
                if grp == 'p':
                    a = sb_attention_prompt(q3, kb3, vb3)
                else:
                    a = sb_attention_sample(q3, kb3, vb3, cache_sb_k[j].reshape(b, past, d),
                                            cache_sb_v[j].reshape(b, past, d))
                a = a.reshape(b * t, d)
            else:
                if grp == 'p':
                    h0 = jnp.zeros((b, 1, D_RNN), F32)
                    c0 = jnp.zeros((b, CONV_W - 1, D_RNN), F32)
                else:
                    h0 = state_lru_h[j].reshape(b, 1, D_RNN)
                    c0 = state_lru_conv[j]
                a, hn, cn = lru_core(x.reshape(b, t, d), g[1], w_in, lru_conv_w[j], vecs[0], wa_bd, vecs[1], wx_bd,
                                     vecs[2], vecs[3], h0, c0)
                new_h[grp].append(hn.reshape(b, D_RNN))
                new_c[grp].append(cn)
                a = a.reshape(b * t, D_RNN)
            xs[grp] = ffn2_ple(x, a, wo, ps[grp][i], g[2], wg2, wu2, wd2, g[3], wpe, wpg, gf,
                               final_norm=(i == depth - 1))

    return (xs['p'].reshape(bp, tp, d), xs['s'].reshape(bs, ts, d),
            jnp.stack(new_k['p']), jnp.stack(new_v['p']), jnp.stack(new_h['p']), jnp.stack(new_c['p']),
            jnp.stack(new_k['s']), jnp.stack(new_v['s']), jnp.stack(new_h['s']), jnp.stack(new_c['s']))
```

```python
import functools
import math

import jax
import jax.numpy as jnp
from jax import lax
from jax.experimental import pallas as pl
from jax.experimental.pallas import tpu as pltpu

D_MODEL = 1024
SB_HEADS = 16
SB_HEAD_DIM = 64
D_RNN = 1280
LRU_BLOCKS = 16
LRU_BLOCK = 80
CONV_W = 4
LRU_C = 8.0
D_FF = 2816
PLE_DIM = 256
EPS = 1e-6
LOG2E = 1.4426950408889634

BF16 = jnp.bfloat16
F32 = jnp.float32

V7X_VMEM_LIMIT = 56 * 1024 * 1024
V7X_MXU_COLS = 256


def _const_spec(shape):
    nd = len(shape)
    return pl.BlockSpec(shape, lambda *_: (0,) * nd, pipeline_mode=pl.Buffered(1))


def _rms(x, g):
    ms = jnp.mean(x * x, axis=-1, keepdims=True)
    return x * lax.rsqrt(ms + EPS) * g


def _dot(a, b):
    return jnp.dot(a, b, preferred_element_type=F32)


def _nt_dot(a, b):
    return lax.dot_general(a, b, (((1,), (1,)), ((), ())), preferred_element_type=F32)


def _tn_dot(a, b):
    return lax.dot_general(a, b, (((0,), (0,)), ((), ())), preferred_element_type=F32)


def _sigmoid(x):
    return 0.5 * jnp.tanh(0.5 * x) + 0.5


def _cparams(sem):
    return pltpu.CompilerParams(dimension_semantics=sem, vmem_limit_bytes=V7X_VMEM_LIMIT)


FF_CHUNK = 256


def _swiglu_half(x, g, wg_ref, wu_ref, wd_ref):
    xn = _rms(x, g).astype(BF16)
    acc = jnp.zeros(x.shape, F32)
    for c in range(D_FF // FF_CHUNK):
        sl = slice(c * FF_CHUNK, (c + 1) * FF_CHUNK)
        gate = _dot(xn, wg_ref[:, sl])
        up = _dot(xn, wu_ref[:, sl])
        h = (gate * _sigmoid(gate) * up).astype(BF16)
        acc = acc + _dot(h, wd_ref[sl, :])
    return x + 0.5 * acc


def _ffn1_kernel(x_ref, g_ref, wg_ref, wu_ref, wd_ref, o_ref):
    o_ref[...] = _swiglu_half(x_ref[...], g_ref[...], wg_ref, wu_ref, wd_ref)


def _ffn2_ple_kernel(x_ref, a_ref, wo_ref, p_ref, g2_ref, wg_ref, wu_ref, wd_ref, g3_ref, wpe_ref, wpg_ref,
                     gf_ref, o_ref, *, final_norm):
    x = x_ref[...] + _dot(a_ref[...], wo_ref[...])
    x = _swiglu_half(x, g2_ref[...], wg_ref, wu_ref, wd_ref)
    xn = _rms(x, g3_ref[...]).astype(BF16)
    gate = _sigmoid(_dot(xn, wpg_ref[...]))
    pe = _dot(p_ref[...].astype(BF16), wpe_ref[...])
    x = x + pe * gate
    if final_norm:
        x = _rms(x, gf_ref[...])
    o_ref[...] = x


def _token_tile(n):
    for tm in (512, 256, 128):
        if n % tm == 0:
            return tm
    raise ValueError(f"token count {n} not a multiple of 128")


def _tok_spec(tm, width):
    return pl.BlockSpec((tm, width), lambda i: (i, 0))


def ffn1(x, g, wg, wu, wd):
    n = x.shape[0]
    tm = _token_tile(n)
    return pl.pallas_call(
        _ffn1_kernel,
        grid=(n // tm,),
        in_specs=[_tok_spec(tm, D_MODEL), _const_spec((1, D_MODEL)), _const_spec(wg.shape), _const_spec(wu.shape),
                  _const_spec(wd.shape)],
        out_specs=_tok_spec(tm, D_MODEL),
        out_shape=jax.ShapeDtypeStruct(x.shape, F32),
        compiler_params=_cparams(("parallel",)),
        name="ffn1",
    )(x, g, wg, wu, wd)


def ffn2_ple(x, a, wo, p_all, layer, g2, wg, wu, wd, g3, wpe, wpg, gf, final_norm):
    n = x.shape[0]
    tm = _token_tile(n)
    ka = a.shape[1]
    p_spec = pl.BlockSpec((None, tm, PLE_DIM), lambda i: (layer, i, 0))
    return pl.pallas_call(
        functools.partial(_ffn2_ple_kernel, final_norm=final_norm),
        grid=(n // tm,),
        in_specs=[_tok_spec(tm, D_MODEL), _tok_spec(tm, ka), _const_spec(wo.shape), p_spec,
                  _const_spec((1, D_MODEL)), _const_spec(wg.shape), _const_spec(wu.shape), _const_spec(wd.shape),
                  _const_spec((1, D_MODEL)), _const_spec(wpe.shape), _const_spec(wpg.shape),
                  _const_spec((1, D_MODEL))],
        out_specs=_tok_spec(tm, D_MODEL),
        out_shape=jax.ShapeDtypeStruct(x.shape, F32),
        compiler_params=_cparams(("parallel",)),
        name="ffn2_ple",
    )(x, a, wo, p_all, g2, wg, wu, wd, g3, wpe, wpg, gf)


Q_SCALE = -(SB_HEAD_DIM ** -0.5) * LOG2E


def _qkv_kernel(x_ref, g_ref, w_ref, q_ref, k_ref, v_ref, kb_ref, vb_ref):
    xn = _rms(x_ref[...], g_ref[...]).astype(BF16)
    d = D_MODEL
    q_ref[...] = (_dot(xn, w_ref[:, 0:d]) * Q_SCALE).astype(BF16)
    k = _dot(xn, w_ref[:, d:2 * d])
    k_ref[...] = k
    kb_ref[...] = k.astype(BF16)
    v = _dot(xn, w_ref[:, 2 * d:3 * d])
    v_ref[...] = v
    vb_ref[...] = v.astype(BF16)


def qkv_proj(x, g, w):
    n = x.shape[0]
    tm = _token_tile(n)
    tok = _tok_spec(tm, D_MODEL)
    return pl.pallas_call(
        _qkv_kernel,
        grid=(n // tm,),
        in_specs=[tok, _const_spec((1, D_MODEL)), _const_spec(w.shape)],
        out_specs=[tok] * 5,
        out_shape=[jax.ShapeDtypeStruct(x.shape, BF16), jax.ShapeDtypeStruct(x.shape, F32),
                   jax.ShapeDtypeStruct(x.shape, F32), jax.ShapeDtypeStruct(x.shape, BF16),
                   jax.ShapeDtypeStruct(x.shape, BF16)],
        compiler_params=_cparams(("parallel",)),
        name="qkv_proj",
    )(x, g, w)


def _qkv_t_kernel(x_ref, g_ref, wq_ref, wkt_ref, wvt_ref, *rest):
    q_ref, kt_ref, vt_ref, ktb_ref, vtb_ref = rest[-5:]
    xn = _rms(x_ref[0], g_ref[...]).astype(BF16)
    q_ref[0] = (_dot(xn, wq_ref[...]) * Q_SCALE).astype(BF16)
    kt = _nt_dot(wkt_ref[...], xn)
    kt_ref[0, 0] = kt
    ktb_ref[0] = kt.astype(BF16)
    vt = _nt_dot(wvt_ref[...], xn)
    vt_ref[0, 0] = vt
    vtb_ref[0] = vt.astype(BF16)


def qkv_proj_t(x, g, wq, wkt, wvt, slot, n_slots, prev=None):
    b, t, d = x.shape
    tm = _token_tile(t)
    tok = pl.BlockSpec((1, tm, d), lambda i, j: (i, j, 0))
    feat = pl.BlockSpec((1, d, tm), lambda i, j: (i, 0, j))
    slot_spec = pl.BlockSpec((1, 1, d, tm), lambda i, j: (slot, i, 0, j))
    in_specs = [tok, _const_spec((1, d)), _const_spec(wq.shape), _const_spec(wkt.shape), _const_spec(wvt.shape)]
    args = [x, g, wq, wkt, wvt]
    aliases = {}
    if prev is not None:
        in_specs += [pl.BlockSpec(memory_space=pl.ANY)] * 2
        aliases = {len(args): 1, len(args) + 1: 2}
        args += list(prev)
    slots = jax.ShapeDtypeStruct((n_slots, b, d, t), F32)
    return pl.pallas_call(
        _qkv_t_kernel,
        grid=(b, t // tm),
        in_specs=in_specs,
        out_specs=[tok, slot_spec, slot_spec, feat, feat],
        out_shape=[jax.ShapeDtypeStruct((b, t, d), BF16), slots, slots,
                   jax.ShapeDtypeStruct((b, d, t), BF16), jax.ShapeDtypeStruct((b, d, t), BF16)],
        input_output_aliases=aliases,
        compiler_params=_cparams(("parallel", "parallel")),
        name="qkv_proj_t",
    )(*args)


ATT_TK = 256
ATT_TQ = 512


def _upper_ones(n):
    r = lax.broadcasted_iota(jnp.int32, (n, n), 0)
    c = lax.broadcasted_iota(jnp.int32, (n, n), 1)
    return jnp.where(c > r, 1.0, 0.0).astype(BF16)


def _log2_stay(nz):
    neg_abs = lax.bitcast_convert_type(lax.bitcast_convert_type(nz, jnp.int32) | jnp.int32(-2 ** 31), F32)
    return jnp.minimum(nz, 0.0) - jnp.log2(1.0 + jnp.exp2(neg_abs))


def _sb_step(nz, visibles, uppers, carries, value_dot):
    n_t, n_g = len(nz), len(nz[0])
    stay = [[None] * n_g for _ in range(n_t)]
    for t in range(n_t):
        for g in range(n_g):
            s = _log2_stay(nz[t][g])
            if visibles[t] is not None:
                s = jnp.where(visibles[t], s, 0.0)
            stay[t][g] = s
    after = [[_dot(uppers[t], stay[t][g].astype(BF16)) for g in range(n_g)] for t in range(n_t)]
    results = []
    for g in range(n_g):
        carry = carries[g]
        out_t = None
        for t in range(n_t):
            w = jnp.exp2((stay[t][g] - nz[t][g]) + after[t][g] + carry)
            if visibles[t] is not None:
                w = jnp.where(visibles[t], w, 0.0)
            contrib = value_dot(t, w.astype(BF16))
            out_t = contrib if out_t is None else out_t + contrib
            carry = carry + after[t][g][0:1, :] + stay[t][g][0:1, :]
        results.append((out_t, carry))
    return results


def _sb_prompt_kernel(q_ref, kt_ref, vt_ref, o_ref, k_ref, acc_ref, carry_ref):
    t = q_ref.shape[1]
    k_ref[...] = kt_ref[0].astype(F32).T.astype(BF16)
    upper = _upper_ones(ATT_TK)
    lane = lax.broadcasted_iota(jnp.int32, (1, 128), 1)
    head_a = lane < SB_HEAD_DIM
    kpos = lax.broadcasted_iota(jnp.int32, (ATT_TK, ATT_TQ), 0)
    qpos = lax.broadcasted_iota(jnp.int32, (ATT_TK, ATT_TQ), 1)
    diag_visible = [kpos + ATT_TK < qpos, kpos < qpos]

    def q_block(qb, _):
        q0 = pl.multiple_of(qb * ATT_TQ, ATT_TQ)
        q2 = q_ref[0, pl.ds(q0, ATT_TQ), :]
        heads = [jnp.where(head_a, q2, jnp.zeros_like(q2)), jnp.where(head_a, jnp.zeros_like(q2), q2)]

        def step(k_hi, visibles, carries):
            starts = [k_hi, k_hi - ATT_TK]
            nz = [[_nt_dot(k_ref[pl.ds(s, ATT_TK), :], qh) for qh in heads] for s in starts]
            v_tiles = [vt_ref[0, :, pl.ds(s, ATT_TK)] for s in starts]
            return _sb_step(nz, visibles, [upper, upper], carries, lambda i, w: _dot(v_tiles[i], w))

        zero = jnp.zeros((1, ATT_TQ), F32)
        res = step(pl.multiple_of(q0 + ATT_TK, ATT_TK), diag_visible, [zero, zero])
        for h, (out_t, carry) in enumerate(res):
            acc_ref[h] = out_t
            carry_ref[h] = jnp.broadcast_to(carry, (8, ATT_TQ))

        def k_block(i, _):
            res = step(pl.multiple_of(q0 - (2 * i + 1) * ATT_TK, ATT_TK), [None, None],
                       [carry_ref[0, 0:1, :], carry_ref[1, 0:1, :]])
            for h, (out_t, carry) in enumerate(res):
                acc_ref[h] += out_t
                carry_ref[h] = jnp.broadcast_to(carry, (8, ATT_TQ))
            return 0

        lax.fori_loop(0, qb * (ATT_TQ // (2 * ATT_TK)), k_block, 0)
        row = lax.broadcasted_iota(jnp.int32, (128, 1), 0)
        out_t = jnp.where(row < SB_HEAD_DIM, acc_ref[0], acc_ref[1])
        o_ref[0, pl.ds(q0, ATT_TQ), :] = out_t.T.astype(BF16)
        return 0

    lax.fori_loop(0, t // ATT_TQ, q_block, 0)


def sb_attention_prompt(q, kt, vt):
    b, t, d = q.shape
    assert ATT_TQ == 2 * ATT_TK and t % ATT_TQ == 0
    tok = pl.BlockSpec((1, t, 128), lambda i, j: (i, 0, j))
    feat = pl.BlockSpec((1, 128, t), lambda i, j: (i, j, 0))
    return pl.pallas_call(
        _sb_prompt_kernel,
        grid=(b, d // 128),
        in_specs=[tok, feat, feat],
        out_specs=tok,
        out_shape=jax.ShapeDtypeStruct(q.shape, BF16),
        scratch_shapes=[pltpu.VMEM((t, 128), BF16), pltpu.VMEM((2, 128, ATT_TQ), F32),
                        pltpu.VMEM((2, 8, ATT_TQ), F32)],
        compiler_params=_cparams(("parallel", "parallel")),
        name="sb_attn_prompt",
    )(q, kt, vt)


SAMPLE_HEADS = 4
SAMPLE_W = SAMPLE_HEADS * SB_HEAD_DIM


def _sb_sample_kernel(q_ref, kn_ref, vn_ref, ckt_ref, cvt_ref, o_ref):
    tq = q_ref.shape[1]
    past = ckt_ref.shape[3]
    lanes = SAMPLE_HEADS * tq
    upper = _upper_ones(ATT_TK)
    q4 = q_ref[0]
    qrows = jnp.concatenate([q4] * SAMPLE_HEADS, axis=0)
    r_head = lax.broadcasted_iota(jnp.int32, (lanes, SAMPLE_W), 0) // tq
    c_head = lax.broadcasted_iota(jnp.int32, (lanes, SAMPLE_W), 1) // SB_HEAD_DIM
    own = r_head == c_head
    qm = jnp.where(own, qrows, jnp.zeros_like(qrows))

    kpos = lax.broadcasted_iota(jnp.int32, (tq, lanes), 0)
    qpos = lax.broadcasted_iota(jnp.int32, (tq, lanes), 1) % tq
    starts = list(range(past - ATT_TK, -1, -ATT_TK))
    k_tiles = [kn_ref[0]] + [ckt_ref[0, 0, :, s:s + ATT_TK].T.astype(BF16) for s in starts]
    nz = [[_nt_dot(kt, qm)] for kt in k_tiles]
    vn = vn_ref[0]
    vt_tiles = [cvt_ref[0, 0, :, s:s + ATT_TK].astype(BF16) for s in starts]

    def value_dot(i, w):
        return _tn_dot(vn, w) if i == 0 else _dot(vt_tiles[i - 1], w)

    visibles = [kpos < qpos] + [None] * len(starts)
    uppers = [upper[0:tq, 0:tq]] + [upper] * len(starts)
    (acc, _), = _sb_step(nz, visibles, uppers, [jnp.zeros((1, lanes), F32)], value_dot)
    acc_t = jnp.where(own, acc.T, 0.0)
    out = acc_t[0:tq]
    for h in range(1, SAMPLE_HEADS):
        out = out + acc_t[h * tq:(h + 1) * tq]
    o_ref[0] = out.astype(BF16)


def sb_attention_sample(q, kn, vn, cache_kt, cache_vt, slot):
    b, tq, d = q.shape
    past = cache_kt.shape[3]
    assert past % ATT_TK == 0
    new_spec = pl.BlockSpec((1, tq, SAMPLE_W), lambda i, j: (i, 0, j))
    cache_spec = pl.BlockSpec((1, 1, SAMPLE_W, past), lambda i, j: (slot, i, j, 0))
    return pl.pallas_call(
        _sb_sample_kernel,
        grid=(b, d // SAMPLE_W),
        in_specs=[new_spec, new_spec, new_spec, cache_spec, cache_spec],
        out_specs=new_spec,
        out_shape=jax.ShapeDtypeStruct(q.shape, BF16),
        compiler_params=_cparams(("parallel", "parallel")),
        name="sb_attn_sample",
    )(q, kn, vn, cache_kt, cache_vt)


TAIL_ROWS = 8
GATE_K = 512


def _gate_tile_starts():
    starts = []
    for n in range(D_RNN // V7X_MXU_COLS):
        first = (n * V7X_MXU_COLS) // LRU_BLOCK * LRU_BLOCK
        last = ((n + 1) * V7X_MXU_COLS - 1) // LRU_BLOCK * LRU_BLOCK + LRU_BLOCK
        lo = min(first // 128 * 128, D_RNN - GATE_K)
        assert lo <= first and last <= lo + GATE_K
        starts.append(lo)
    return starts


def _gate_slabs(w_a, w_x):
    def dense(w):
        eye = jnp.eye(LRU_BLOCKS, dtype=w.dtype)
        return jnp.einsum('njk,nm->njmk', w, eye).reshape(D_RNN, D_RNN)
    da, dx = dense(w_a), dense(w_x)
    slabs = []
    for n, lo in enumerate(_gate_tile_starts()):
        cols = slice(n * V7X_MXU_COLS, (n + 1) * V7X_MXU_COLS)
        slabs.append(jnp.concatenate([da[lo:lo + GATE_K, cols], dx[lo:lo + GATE_K, cols]], axis=1))
    return jnp.stack(slabs).astype(BF16)


def _gelu_tanh(x):
    return 0.5 * x * (1.0 + jnp.tanh(math.sqrt(2.0 / math.pi) * (x + 0.044715 * (x * x * x))))


def _lru_kernel(x_ref, g_ref, win_ref, cw_ref, cb_ref, wg_ref, ba_ref, bx_ref, lam_ref, h0_ref, c0_ref,
                y_ref, hn_ref, cn_ref, ext_ref, a_ref, b_ref, h_ref, hstate_ref):
    tc = x_ref.shape[1]
    step = pl.program_id(1)

    @pl.when(step == 0)
    def _():
        hstate_ref[...] = h0_ref[0]
        ext_ref[0:TAIL_ROWS - (CONV_W - 1), :] = jnp.zeros((TAIL_ROWS - (CONV_W - 1), D_RNN), F32)
        ext_ref[TAIL_ROWS - (CONV_W - 1):TAIL_ROWS, :] = c0_ref[0]

    xn = _rms(x_ref[0], g_ref[...]).astype(BF16)
    gx = _dot(xn, win_ref[...])
    gate = _gelu_tanh(gx[:, :D_RNN])
    u = gx[:, D_RNN:]
    ext_ref[TAIL_ROWS:, :] = u
    c = cb_ref[...] + u * cw_ref[CONV_W - 1:CONV_W, :]
    for j in range(CONV_W - 1):
        shift = CONV_W - 1 - j
        c = c + ext_ref[TAIL_ROWS - shift:TAIL_ROWS - shift + tc, :] * cw_ref[j:j + 1, :]
    new_tail = ext_ref[tc + TAIL_ROWS - (CONV_W - 1):tc + TAIL_ROWS, :]
    ext_ref[TAIL_ROWS - (CONV_W - 1):TAIL_ROWS, :] = new_tail
    cn_ref[0] = new_tail

    cbf = c.astype(BF16)
    pre = [_dot(cbf[:, lo:lo + GATE_K], wg_ref[n]) for n, lo in enumerate(_gate_tile_starts())]
    r = _sigmoid(jnp.concatenate([p[:, :V7X_MXU_COLS] for p in pre], axis=-1) + ba_ref[...])
    gin = _sigmoid(jnp.concatenate([p[:, V7X_MXU_COLS:] for p in pre], axis=-1) + bx_ref[...])
    lam = lam_ref[...]
    softplus_neg_lam = jnp.maximum(-lam, 0.0) + jnp.log(1.0 + jnp.exp(-jnp.abs(lam)))
    log_a = (-LRU_C * softplus_neg_lam) * r
    a = jnp.exp(log_a)
    a_ref[...] = a
    b_ref[...] = jnp.sqrt(-jnp.tanh(log_a) * (a * a + 1.0)) * (gin * c)

    def scan_step(i, h):
        h = a_ref[pl.ds(i, 1), :] * h + b_ref[pl.ds(i, 1), :]
        h_ref[pl.ds(i, 1), :] = h
        return h

    h_last = lax.fori_loop(0, tc, scan_step, hstate_ref[...], unroll=8)
    hstate_ref[...] = h_last
    hn_ref[0] = h_last
    y_ref[0] = (h_ref[...] * gate).astype(BF16)


def lru_core(x, g, w_in, conv_w, conv_b, w_gates, b_a, b_x, lam, h0, conv0):
    b, t, d = x.shape
    tc = 256 if t % 256 == 0 else t
    vec = _const_spec((1, D_RNN))
    return pl.pallas_call(
        _lru_kernel,
        grid=(b, t // tc),
        in_specs=[pl.BlockSpec((1, tc, d), lambda i, j: (i, j, 0)), _const_spec((1, D_MODEL)),
                  _const_spec(w_in.shape), _const_spec((CONV_W, D_RNN)), vec, _const_spec(w_gates.shape), vec,
                  vec, vec,
                  pl.BlockSpec((1, 1, D_RNN), lambda i, j: (i, 0, 0)),
                  pl.BlockSpec((1, CONV_W - 1, D_RNN), lambda i, j: (i, 0, 0))],
        out_specs=[pl.BlockSpec((1, tc, D_RNN), lambda i, j: (i, j, 0)),
                   pl.BlockSpec((1, 1, D_RNN), lambda i, j: (i, 0, 0)),
                   pl.BlockSpec((1, CONV_W - 1, D_RNN), lambda i, j: (i, 0, 0))],
        out_shape=[jax.ShapeDtypeStruct((b, t, D_RNN), BF16), jax.ShapeDtypeStruct((b, 1, D_RNN), F32),
                   jax.ShapeDtypeStruct((b, CONV_W - 1, D_RNN), F32)],
        scratch_shapes=[pltpu.VMEM((tc + TAIL_ROWS, D_RNN), F32), pltpu.VMEM((tc, D_RNN), F32),
                        pltpu.VMEM((tc, D_RNN), F32), pltpu.VMEM((tc, D_RNN), F32), pltpu.VMEM((1, D_RNN), F32)],
        compiler_params=_cparams(("parallel", "arbitrary")),
        name="lru_core",
    )(x, g, w_in, conv_w, conv_b, w_gates, b_a, b_x, lam, h0, conv0)


def kernel(x_prompt, x_sample, p_prompt, p_sample, cache_sb_k, cache_sb_v, state_lru_h, state_lru_conv, norm_g,
           ffn_w_gate, ffn_w_up, ffn_w_down, ple_w_proj, ple_w_gate, sb_w_qkv, sb_w_o, lru_w_in, lru_conv_w,
           lru_conv_b, lru_w_a, lru_b_a, lru_w_x, lru_b_x, lru_lambda, lru_w_o, final_norm_g):
    depth = norm_g.shape[0]
    bp, tp, d = x_prompt.shape
    bs, ts, _ = x_sample.shape
    n_sb, _, past = cache_sb_k.shape[:3]
    xs = {'p': x_prompt.reshape(bp * tp, d), 's': x_sample.reshape(bs * ts, d)}
    ps = {'p': p_prompt.reshape(depth, bp * tp, PLE_DIM), 's': p_sample.reshape(depth, bs * ts, PLE_DIM)}
    dims = {'p': (bp, tp), 's': (bs, ts)}
    gf = final_norm_g.reshape(1, d)
    cache_kt = jnp.transpose(cache_sb_k, (0, 1, 3, 4, 2)).reshape(n_sb, bs, d, past)
    cache_vt = jnp.transpose(cache_sb_v, (0, 1, 3, 4, 2)).reshape(n_sb, bs, d, past)
    kv_prompt = None
    new_ks, new_vs = [], []
    new_h = {'p': [], 's': []}
    new_c = {'p': [], 's': []}

    for i in range(depth):
        j = i // 2
        g = norm_g[i].reshape(4, 1, d)
        wg1, wu1, wd1 = (w[i, 0].astype(BF16) for w in (ffn_w_gate, ffn_w_up, ffn_w_down))
        wg2, wu2, wd2 = (w[i, 1].astype(BF16) for w in (ffn_w_gate, ffn_w_up, ffn_w_down))
        wpe = ple_w_proj[i].astype(BF16)
        wpg = ple_w_gate[i].astype(BF16)
        if i % 2 == 0:
            wqkv = sb_w_qkv[j].astype(BF16)
            wq, wkt, wvt = wqkv[:, :d], wqkv[:, d:2 * d].T, wqkv[:, 2 * d:].T
            wo = sb_w_o[j].astype(BF16)
        else:
            w_in = lru_w_in[j].astype(BF16)
            w_gates = _gate_slabs(lru_w_a[j], lru_w_x[j])
            wo = lru_w_o[j].astype(BF16)
            vecs = [v[j].reshape(1, D_RNN) for v in (lru_conv_b, lru_b_a, lru_b_x, lru_lambda)]
        for grp in ('p', 's'):
            b, t = dims[grp]
            x = ffn1(xs[grp], g[0], wg1, wu1, wd1)
            if i % 2 == 0 and grp == 'p':
                q, kt, vt, ktb, vtb = qkv_proj_t(x.reshape(b, t, d), g[1], wq, wkt, wvt, j, n_sb, kv_prompt)
                kv_prompt = (kt, vt)
                a = sb_attention_prompt(q, ktb, vtb).reshape(b * t, d)
            elif i % 2 == 0:
                q, k, v, kb, vb = qkv_proj(x, g[1], wqkv)
                new_ks.append(k.reshape(b, t, SB_HEADS, SB_HEAD_DIM))
                new_vs.append(v.reshape(b, t, SB_HEADS, SB_HEAD_DIM))
                q3, kb3, vb3 = (arr.reshape(b, t, d) for arr in (q, kb, vb))
                a = sb_attention_sample(q3, kb3, vb3, cache_kt, cache_vt, j).reshape(b * t, d)
            else:
                if grp == 'p':
                    h0 = jnp.zeros((b, 1, D_RNN), F32)
                    c0 = jnp.zeros((b, CONV_W - 1, D_RNN), F32)
                else:
                    h0 = state_lru_h[j].reshape(b, 1, D_RNN)
                    c0 = state_lru_conv[j]
                a, hn, cn = lru_core(x.reshape(b, t, d), g[1], w_in, lru_conv_w[j], vecs[0], w_gates, vecs[1],
                                     vecs[2], vecs[3], h0, c0)
                new_h[grp].append(hn.reshape(b, D_RNN))
                new_c[grp].append(cn)
                a = a.reshape(b * t, D_RNN)
            xs[grp] = ffn2_ple(x, a, wo, ps[grp], i, g[2], wg2, wu2, wd2, g[3], wpe, wpg, gf,
                               final_norm=(i == depth - 1))

    def time_major(kvt):
        return jnp.transpose(kvt.reshape(n_sb, bp, SB_HEADS, SB_HEAD_DIM, tp), (0, 1, 4, 2, 3))

    return (xs['p'].reshape(bp, tp, d), xs['s'].reshape(bs, ts, d),
            time_major(kv_prompt[0]), time_major(kv_prompt[1]), jnp.stack(new_h['p']), jnp.stack(new_c['p']),
            jnp.stack(new_ks), jnp.stack(new_vs), jnp.stack(new_h['s']), jnp.stack(new_c['s']))
```

```python
import functools
import math

import jax
import jax.numpy as jnp
from jax import lax
from jax.experimental import pallas as pl
from jax.experimental.pallas import tpu as pltpu

D_MODEL = 1024
SB_HEADS = 16
SB_HEAD_DIM = 64
D_RNN = 1280
LRU_BLOCKS = 16
LRU_BLOCK = 80
CONV_W = 4
LRU_C = 8.0
D_FF = 2816
PLE_DIM = 256
EPS = 1e-6
LOG2E = 1.4426950408889634

BF16 = jnp.bfloat16
F32 = jnp.float32

V7X_VMEM_LIMIT = 56 * 1024 * 1024
V7X_MXU_COLS = 256


def _const_spec(shape):
    nd = len(shape)
    return pl.BlockSpec(shape, lambda *_: (0,) * nd, pipeline_mode=pl.Buffered(1))


def _rms(x, g):
    ms = jnp.mean(x * x, axis=-1, keepdims=True)
    return x * lax.rsqrt(ms + EPS) * g


def _dot(a, b):
    return jnp.dot(a, b, preferred_element_type=F32)


def _nt_dot(a, b):
    return lax.dot_general(a, b, (((1,), (1,)), ((), ())), preferred_element_type=F32)


def _tn_dot(a, b):
    return lax.dot_general(a, b, (((0,), (0,)), ((), ())), preferred_element_type=F32)


def _sigmoid(x):
    return 0.5 * jnp.tanh(0.5 * x) + 0.5


def _cparams(sem):
    return pltpu.CompilerParams(dimension_semantics=sem, vmem_limit_bytes=V7X_VMEM_LIMIT)


FF_CHUNK = 256


def _swiglu_half(x, g, wg_ref, wu_ref, wd_ref):
    xn = _rms(x, g).astype(BF16)
    acc = jnp.zeros(x.shape, F32)
    for c in range(D_FF // FF_CHUNK):
        sl = slice(c * FF_CHUNK, (c + 1) * FF_CHUNK)
        gate = _dot(xn, wg_ref[:, sl])
        up = _dot(xn, wu_ref[:, sl])
        h = (gate * _sigmoid(gate) * up).astype(BF16)
        acc = acc + _dot(h, wd_ref[sl, :])
    return x + 0.5 * acc


def _ffn1_kernel(x_ref, g_ref, wg_ref, wu_ref, wd_ref, o_ref):
    o_ref[...] = _swiglu_half(x_ref[...], g_ref[...], wg_ref, wu_ref, wd_ref)


def _ffn2_ple_kernel(x_ref, a_ref, wo_ref, p_ref, g2_ref, wg_ref, wu_ref, wd_ref, g3_ref, wpe_ref, wpg_ref,
                     gf_ref, o_ref, *, final_norm):
    x = x_ref[...] + _dot(a_ref[...], wo_ref[...])
    x = _swiglu_half(x, g2_ref[...], wg_ref, wu_ref, wd_ref)
    xn = _rms(x, g3_ref[...]).astype(BF16)
    gate = _sigmoid(_dot(xn, wpg_ref[...]))
    pe = _dot(p_ref[...].astype(BF16), wpe_ref[...])
    x = x + pe * gate
    if final_norm:
        x = _rms(x, gf_ref[...])
    o_ref[...] = x


def _token_tile(n):
    for tm in (512, 256, 128):
        if n % tm == 0:
            return tm
    raise ValueError(f"token count {n} not a multiple of 128")


def _tok_spec(tm, width):
    return pl.BlockSpec((tm, width), lambda i: (i, 0))


def ffn1(x, g, wg, wu, wd):
    n = x.shape[0]
    tm = _token_tile(n)
    return pl.pallas_call(
        _ffn1_kernel,
        grid=(n // tm,),
        in_specs=[_tok_spec(tm, D_MODEL), _const_spec((1, D_MODEL)), _const_spec(wg.shape), _const_spec(wu.shape),
                  _const_spec(wd.shape)],
        out_specs=_tok_spec(tm, D_MODEL),
        out_shape=jax.ShapeDtypeStruct(x.shape, F32),
        compiler_params=_cparams(("parallel",)),
        name="ffn1",
    )(x, g, wg, wu, wd)


def ffn2_ple(x, a, wo, p_all, layer, g2, wg, wu, wd, g3, wpe, wpg, gf, final_norm):
    n = x.shape[0]
    tm = _token_tile(n)
    ka = a.shape[1]
    p_spec = pl.BlockSpec((None, tm, PLE_DIM), lambda i: (layer, i, 0))
    return pl.pallas_call(
        functools.partial(_ffn2_ple_kernel, final_norm=final_norm),
        grid=(n // tm,),
        in_specs=[_tok_spec(tm, D_MODEL), _tok_spec(tm, ka), _const_spec(wo.shape), p_spec,
                  _const_spec((1, D_MODEL)), _const_spec(wg.shape), _const_spec(wu.shape), _const_spec(wd.shape),
                  _const_spec((1, D_MODEL)), _const_spec(wpe.shape), _const_spec(wpg.shape),
                  _const_spec((1, D_MODEL))],
        out_specs=_tok_spec(tm, D_MODEL),
        out_shape=jax.ShapeDtypeStruct(x.shape, F32),
        compiler_params=_cparams(("parallel",)),
        name="ffn2_ple",
    )(x, a, wo, p_all, g2, wg, wu, wd, g3, wpe, wpg, gf)


Q_SCALE = -(SB_HEAD_DIM ** -0.5) * LOG2E


def _qkv_kernel(x_ref, g_ref, w_ref, q_ref, k_ref, v_ref, kb_ref, vb_ref):
    xn = _rms(x_ref[...], g_ref[...]).astype(BF16)
    d = D_MODEL
    q_ref[...] = (_dot(xn, w_ref[:, 0:d]) * Q_SCALE).astype(BF16)
    k = _dot(xn, w_ref[:, d:2 * d])
    k_ref[...] = k
    kb_ref[...] = k.astype(BF16)
    v = _dot(xn, w_ref[:, 2 * d:3 * d])
    v_ref[...] = v
    vb_ref[...] = v.astype(BF16)


def qkv_proj(x, g, w):
    n = x.shape[0]
    tm = _token_tile(n)
    tok = _tok_spec(tm, D_MODEL)
    return pl.pallas_call(
        _qkv_kernel,
        grid=(n // tm,),
        in_specs=[tok, _const_spec((1, D_MODEL)), _const_spec(w.shape)],
        out_specs=[tok] * 5,
        out_shape=[jax.ShapeDtypeStruct(x.shape, BF16), jax.ShapeDtypeStruct(x.shape, F32),
                   jax.ShapeDtypeStruct(x.shape, F32), jax.ShapeDtypeStruct(x.shape, BF16),
                   jax.ShapeDtypeStruct(x.shape, BF16)],
        compiler_params=_cparams(("parallel",)),
        name="qkv_proj",
    )(x, g, w)


def _qkv_t_kernel(x_ref, g_ref, wq_ref, wkt_ref, wvt_ref, *rest):
    q_ref, kt_ref, vt_ref, ktb_ref, vtb_ref = rest[-5:]
    xn = _rms(x_ref[0], g_ref[...]).astype(BF16)
    q_ref[0] = (_dot(xn, wq_ref[...]) * Q_SCALE).astype(BF16)
    kt = _nt_dot(wkt_ref[...], xn)
    kt_ref[0, 0] = kt
    ktb_ref[0] = kt.astype(BF16)
    vt = _nt_dot(wvt_ref[...], xn)
    vt_ref[0, 0] = vt
    vtb_ref[0] = vt.astype(BF16)


def qkv_proj_t(x, g, wq, wkt, wvt, slot, n_slots, prev=None):
    b, t, d = x.shape
    tm = _token_tile(t)
    tok = pl.BlockSpec((1, tm, d), lambda i, j: (i, j, 0))
    feat = pl.BlockSpec((1, d, tm), lambda i, j: (i, 0, j))
    slot_spec = pl.BlockSpec((1, 1, d, tm), lambda i, j: (slot, i, 0, j))
    in_specs = [tok, _const_spec((1, d)), _const_spec(wq.shape), _const_spec(wkt.shape), _const_spec(wvt.shape)]
    args = [x, g, wq, wkt, wvt]
    aliases = {}
    if prev is not None:
        in_specs += [pl.BlockSpec(memory_space=pl.ANY)] * 2
        aliases = {len(args): 1, len(args) + 1: 2}
        args += list(prev)
    slots = jax.ShapeDtypeStruct((n_slots, b, d, t), F32)
    return pl.pallas_call(
        _qkv_t_kernel,
        grid=(b, t // tm),
        in_specs=in_specs,
        out_specs=[tok, slot_spec, slot_spec, feat, feat],
        out_shape=[jax.ShapeDtypeStruct((b, t, d), BF16), slots, slots,
                   jax.ShapeDtypeStruct((b, d, t), BF16), jax.ShapeDtypeStruct((b, d, t), BF16)],
        input_output_aliases=aliases,
        compiler_params=_cparams(("parallel", "parallel")),
        name="qkv_proj_t",
    )(*args)


ATT_TK = 256
ATT_TQ = 512


def _upper_ones(n):
    r = lax.broadcasted_iota(jnp.int32, (n, n), 0)
    c = lax.broadcasted_iota(jnp.int32, (n, n), 1)
    return jnp.where(c > r, 1.0, 0.0).astype(BF16)


def _log2_stay(nz):
    neg_abs = lax.bitcast_convert_type(lax.bitcast_convert_type(nz, jnp.int32) | jnp.int32(-2 ** 31), F32)
    return jnp.minimum(nz, 0.0) - jnp.log2(1.0 + jnp.exp2(neg_abs))


STICK_FLOOR = -160.0


def _sb_weights(nz, visibles, uppers, carries):
    n_t, n_g = len(nz), len(nz[0])
    stay = [[None] * n_g for _ in range(n_t)]
    after = [[None] * n_g for _ in range(n_t)]
    for t in range(n_t):
        for g in range(n_g):
            if nz[t][g] is None:
                continue
            s = _log2_stay(nz[t][g])
            if visibles[t][g] is not None:
                s = jnp.where(visibles[t][g], s, 0.0)
            stay[t][g] = s
            after[t][g] = _dot(uppers[t], s.astype(BF16))
    weights = [[None] * n_g for _ in range(n_t)]
    new_carries = []
    for g in range(n_g):
        carry = carries[g]
        for t in range(n_t):
            if nz[t][g] is None:
                continue
            w = jnp.exp2((stay[t][g] - nz[t][g]) + after[t][g] + carry)
            if visibles[t][g] is not None:
                w = jnp.where(visibles[t][g], w, 0.0)
            weights[t][g] = w.astype(BF16)
            carry = carry + after[t][g][0:1, :] + stay[t][g][0:1, :]
        new_carries.append(carry)
    return weights, new_carries


def _sb_prompt_kernel(q_ref, kt_ref, vt_ref, o_ref, k_ref, acc_ref, carry_ref):
    t = q_ref.shape[1]
    half = ATT_TQ // 2
    assert half == ATT_TK
    k_ref[...] = kt_ref[0].astype(F32).T.astype(BF16)
    upper = _upper_ones(ATT_TK)
    lane = lax.broadcasted_iota(jnp.int32, (1, 128), 1)
    head_a = lane < SB_HEAD_DIM
    kpos = lax.broadcasted_iota(jnp.int32, (ATT_TK, half), 0)
    qpos = lax.broadcasted_iota(jnp.int32, (ATT_TK, half), 1)
    causal = kpos < qpos
    row = lax.broadcasted_iota(jnp.int32, (128, 1), 0)

    def k_tile(start):
        return k_ref[pl.ds(pl.multiple_of(start, ATT_TK), ATT_TK), :]

    def v_tile(start):
        return vt_ref[0, :, pl.ds(pl.multiple_of(start, ATT_TK), ATT_TK)]

    def stick_left(carries):
        m = carries[0]
        for c in carries[1:]:
            m = jnp.maximum(m, c)
        return jnp.max(m) > STICK_FLOOR

    def q_block(qb, with_prev):
        q0 = pl.multiple_of(qb * ATT_TQ, ATT_TQ)
        q2 = q_ref[0, pl.ds(q0, ATT_TQ), :]
        heads = [jnp.where(head_a, q2, jnp.zeros_like(q2)), jnp.where(head_a, jnp.zeros_like(q2), q2)]
        groups = [qh[half:] for qh in heads] + [qh[0:half] for qh in heads]
        starts = [q0 + half, q0] + ([q0 - ATT_TK] if with_prev else [])
        k_tiles = [k_tile(s) for s in starts]
        nz = [[_nt_dot(k_tiles[0], g) for g in groups[:2]] + [None, None]]
        nz += [[_nt_dot(kt, g) for g in groups] for kt in k_tiles[1:]]
        visibles = [[causal, causal, None, None], [None, None, causal, causal], [None] * 4][:len(starts)]
        zero = jnp.zeros((1, half), F32)
        w, carries = _sb_weights(nz, visibles, [upper] * len(starts), [zero] * 4)
        v_tiles = [v_tile(s) for s in starts]
        for h in range(2):
            out_hi = _dot(v_tiles[0], w[0][h])
            out_lo = None
            for ti in range(1, len(starts)):
                out_hi = out_hi + _dot(v_tiles[ti], w[ti][h])
                lo = _dot(v_tiles[ti], w[ti][2 + h])
                out_lo = lo if out_lo is None else out_lo + lo
            acc_ref[h, :, 0:half] = out_lo
            acc_ref[h, :, half:] = out_hi
            carry_ref[h] = jnp.broadcast_to(jnp.concatenate([carries[2 + h], carries[h]], axis=1), (8, ATT_TQ))

        if with_prev:
            def more_keys(state):
                i, live = state
                return jnp.logical_and(i < 2 * qb - 1, live)

            def k_block(state):
                i, _ = state
                start = q0 - (i + 2) * ATT_TK
                nz = [[_nt_dot(k_tile(start), qh) for qh in heads]]
                w, carries = _sb_weights(nz, [[None, None]], [upper], [carry_ref[0, 0:1, :], carry_ref[1, 0:1, :]])
                for h in range(2):
                    acc_ref[h] += _dot(v_tile(start), w[0][h])
                    carry_ref[h] = jnp.broadcast_to(carries[h], (8, ATT_TQ))
                return i + 1, stick_left(carries)

            lax.while_loop(more_keys, k_block, (jnp.int32(0), stick_left(carries)))

        out_t = jnp.where(row < SB_HEAD_DIM, acc_ref[0], acc_ref[1])
        o_ref[0, pl.ds(q0, ATT_TQ), :] = out_t.T.astype(BF16)

    q_block(0, False)

    def later_block(qb, _):
        q_block(qb, True)
        return 0

    lax.fori_loop(1, t // ATT_TQ, later_block, 0)


def sb_attention_prompt(q, kt, vt):
    b, t, d = q.shape
    assert ATT_TQ == 2 * ATT_TK and t % ATT_TQ == 0
    tok = pl.BlockSpec((1, t, 128), lambda i, j: (i, 0, j))
    feat = pl.BlockSpec((1, 128, t), lambda i, j: (i, j, 0))
    return pl.pallas_call(
        _sb_prompt_kernel,
        grid=(b, d // 128),
        in_specs=[tok, feat, feat],
        out_specs=tok,
        out_shape=jax.ShapeDtypeStruct(q.shape, BF16),
        scratch_shapes=[pltpu.VMEM((t, 128), BF16), pltpu.VMEM((2, 128, ATT_TQ), F32),
                        pltpu.VMEM((2, 8, ATT_TQ), F32)],
        compiler_params=_cparams(("parallel", "parallel")),
        name="sb_attn_prompt",
    )(q, kt, vt)


SAMPLE_HEADS = 4
SAMPLE_W = SAMPLE_HEADS * SB_HEAD_DIM


def _sb_sample_kernel(q_ref, kn_ref, vn_ref, ckt_ref, cvt_ref, o_ref):
    tq = q_ref.shape[1]
    past = ckt_ref.shape[3]
    lanes = SAMPLE_HEADS * tq
    upper = _upper_ones(ATT_TK)
    q4 = q_ref[0]
    qrows = jnp.concatenate([q4] * SAMPLE_HEADS, axis=0)
    r_head = lax.broadcasted_iota(jnp.int32, (lanes, SAMPLE_W), 0) // tq
    c_head = lax.broadcasted_iota(jnp.int32, (lanes, SAMPLE_W), 1) // SB_HEAD_DIM
    own = r_head == c_head
    qm = jnp.where(own, qrows, jnp.zeros_like(qrows))

    kpos = lax.broadcasted_iota(jnp.int32, (tq, lanes), 0)
    qpos = lax.broadcasted_iota(jnp.int32, (tq, lanes), 1) % tq
    starts = list(range(past - ATT_TK, -1, -ATT_TK))
    k_tiles = [kn_ref[0]] + [ckt_ref[0, 0, :, s:s + ATT_TK].T.astype(BF16) for s in starts]
    nz = [[_nt_dot(kt, qm)] for kt in k_tiles]
    visibles = [[kpos < qpos]] + [[None]] * len(starts)
    uppers = [upper[0:tq, 0:tq]] + [upper] * len(starts)
    w, _ = _sb_weights(nz, visibles, uppers, [jnp.zeros((1, lanes), F32)])
    acc = _tn_dot(vn_ref[0], w[0][0])
    for i, s in enumerate(starts):
        acc = acc + _dot(cvt_ref[0, 0, :, s:s + ATT_TK].astype(BF16), w[i + 1][0])
    acc_t = jnp.where(own, acc.T, 0.0)
    out = acc_t[0:tq]
    for h in range(1, SAMPLE_HEADS):
        out = out + acc_t[h * tq:(h + 1) * tq]
    o_ref[0] = out.astype(BF16)


def sb_attention_sample(q, kn, vn, cache_kt, cache_vt, slot):
    b, tq, d = q.shape
    past = cache_kt.shape[3]
    assert past % ATT_TK == 0
    new_spec = pl.BlockSpec((1, tq, SAMPLE_W), lambda i, j: (i, 0, j))
    cache_spec = pl.BlockSpec((1, 1, SAMPLE_W, past), lambda i, j: (slot, i, j, 0))
    return pl.pallas_call(
        _sb_sample_kernel,
        grid=(b, d // SAMPLE_W),
        in_specs=[new_spec, new_spec, new_spec, cache_spec, cache_spec],
        out_specs=new_spec,
        out_shape=jax.ShapeDtypeStruct(q.shape, BF16),
        compiler_params=_cparams(("parallel", "parallel")),
        name="sb_attn_sample",
    )(q, kn, vn, cache_kt, cache_vt)


TAIL_ROWS = 8
GATE_K = 512


def _gate_tile_starts():
    starts = []
    for n in range(D_RNN // V7X_MXU_COLS):
        first = (n * V7X_MXU_COLS) // LRU_BLOCK * LRU_BLOCK
        last = ((n + 1) * V7X_MXU_COLS - 1) // LRU_BLOCK * LRU_BLOCK + LRU_BLOCK
        lo = min(first // 128 * 128, D_RNN - GATE_K)
        assert lo <= first and last <= lo + GATE_K
        starts.append(lo)
    return starts


def _gate_slabs(w_a, w_x):
    def dense(w):
        eye = jnp.eye(LRU_BLOCKS, dtype=w.dtype)
        return jnp.einsum('njk,nm->njmk', w, eye).reshape(D_RNN, D_RNN)
    da, dx = dense(w_a), dense(w_x)
    slabs = []
    for n, lo in enumerate(_gate_tile_starts()):
        cols = slice(n * V7X_MXU_COLS, (n + 1) * V7X_MXU_COLS)
        slabs.append(jnp.concatenate([da[lo:lo + GATE_K, cols], dx[lo:lo + GATE_K, cols]], axis=1))
    return jnp.stack(slabs).astype(BF16)


def _gelu_tanh(x):
    return 0.5 * x * (1.0 + jnp.tanh(math.sqrt(2.0 / math.pi) * (x + 0.044715 * (x * x * x))))


def _lru_kernel(x_ref, g_ref, win_ref, cw_ref, cb_ref, wg_ref, ba_ref, bx_ref, lam_ref, h0_ref, c0_ref,
                y_ref, hn_ref, cn_ref, ext_ref, a_ref, b_ref, h_ref, hstate_ref):
    tc = x_ref.shape[1]
    step = pl.program_id(1)

    @pl.when(step == 0)
    def _():
        hstate_ref[...] = h0_ref[0]
        ext_ref[0:TAIL_ROWS - (CONV_W - 1), :] = jnp.zeros((TAIL_ROWS - (CONV_W - 1), D_RNN), F32)
        ext_ref[TAIL_ROWS - (CONV_W - 1):TAIL_ROWS, :] = c0_ref[0]

    xn = _rms(x_ref[0], g_ref[...]).astype(BF16)
    gx = _dot(xn, win_ref[...])
    gate = _gelu_tanh(gx[:, :D_RNN])
    u = gx[:, D_RNN:]
    ext_ref[TAIL_ROWS:, :] = u
    c = cb_ref[...] + u * cw_ref[CONV_W - 1:CONV_W, :]
    for j in range(CONV_W - 1):
        shift = CONV_W - 1 - j
        c = c + ext_ref[TAIL_ROWS - shift:TAIL_ROWS - shift + tc, :] * cw_ref[j:j + 1, :]
    new_tail = ext_ref[tc + TAIL_ROWS - (CONV_W - 1):tc + TAIL_ROWS, :]
    ext_ref[TAIL_ROWS - (CONV_W - 1):TAIL_ROWS, :] = new_tail
    cn_ref[0] = new_tail

    cbf = c.astype(BF16)
    pre = [_dot(cbf[:, lo:lo + GATE_K], wg_ref[n]) for n, lo in enumerate(_gate_tile_starts())]
    r = _sigmoid(jnp.concatenate([p[:, :V7X_MXU_COLS] for p in pre], axis=-1) + ba_ref[...])
    gin = _sigmoid(jnp.concatenate([p[:, V7X_MXU_COLS:] for p in pre], axis=-1) + bx_ref[...])
    lam = lam_ref[...]
    softplus_neg_lam = jnp.maximum(-lam, 0.0) + jnp.log(1.0 + jnp.exp(-jnp.abs(lam)))
    log_a = (-LRU_C * softplus_neg_lam) * r
    a = jnp.exp(log_a)
    a_ref[...] = a
    b_ref[...] = jnp.sqrt(-jnp.tanh(log_a) * (a * a + 1.0)) * (gin * c)

    def scan_step(i, h):
        h = a_ref[pl.ds(i, 1), :] * h + b_ref[pl.ds(i, 1), :]
        h_ref[pl.ds(i, 1), :] = h
        return h

    h_last = lax.fori_loop(0, tc, scan_step, hstate_ref[...], unroll=8)
    hstate_ref[...] = h_last
    hn_ref[0] = h_last
    y_ref[0] = (h_ref[...] * gate).astype(BF16)


def lru_core(x, g, w_in, conv_w, conv_b, w_gates, b_a, b_x, lam, h0, conv0):
    b, t, d = x.shape
    tc = 256 if t % 256 == 0 else t
    vec = _const_spec((1, D_RNN))
    return pl.pallas_call(
        _lru_kernel,
        grid=(b, t // tc),
        in_specs=[pl.BlockSpec((1, tc, d), lambda i, j: (i, j, 0)), _const_spec((1, D_MODEL)),
                  _const_spec(w_in.shape), _const_spec((CONV_W, D_RNN)), vec, _const_spec(w_gates.shape), vec,
                  vec, vec,
                  pl.BlockSpec((1, 1, D_RNN), lambda i, j: (i, 0, 0)),
                  pl.BlockSpec((1, CONV_W - 1, D_RNN), lambda i, j: (i, 0, 0))],
        out_specs=[pl.BlockSpec((1, tc, D_RNN), lambda i, j: (i, j, 0)),
                   pl.BlockSpec((1, 1, D_RNN), lambda i, j: (i, 0, 0)),
                   pl.BlockSpec((1, CONV_W - 1, D_RNN), lambda i, j: (i, 0, 0))],
        out_shape=[jax.ShapeDtypeStruct((b, t, D_RNN), BF16), jax.ShapeDtypeStruct((b, 1, D_RNN), F32),
                   jax.ShapeDtypeStruct((b, CONV_W - 1, D_RNN), F32)],
        scratch_shapes=[pltpu.VMEM((tc + TAIL_ROWS, D_RNN), F32), pltpu.VMEM((tc, D_RNN), F32),
                        pltpu.VMEM((tc, D_RNN), F32), pltpu.VMEM((tc, D_RNN), F32), pltpu.VMEM((1, D_RNN), F32)],
        compiler_params=_cparams(("parallel", "arbitrary")),
        name="lru_core",
    )(x, g, w_in, conv_w, conv_b, w_gates, b_a, b_x, lam, h0, conv0)


def kernel(x_prompt, x_sample, p_prompt, p_sample, cache_sb_k, cache_sb_v, state_lru_h, state_lru_conv, norm_g,
           ffn_w_gate, ffn_w_up, ffn_w_down, ple_w_proj, ple_w_gate, sb_w_qkv, sb_w_o, lru_w_in, lru_conv_w,
           lru_conv_b, lru_w_a, lru_b_a, lru_w_x, lru_b_x, lru_lambda, lru_w_o, final_norm_g):
    depth = norm_g.shape[0]
    bp, tp, d = x_prompt.shape
    bs, ts, _ = x_sample.shape
    n_sb, _, past = cache_sb_k.shape[:3]
    xs = {'p': x_prompt.reshape(bp * tp, d), 's': x_sample.reshape(bs * ts, d)}
    ps = {'p': p_prompt.reshape(depth, bp * tp, PLE_DIM), 's': p_sample.reshape(depth, bs * ts, PLE_DIM)}
    dims = {'p': (bp, tp), 's': (bs, ts)}
    gf = final_norm_g.reshape(1, d)
    cache_kt = jnp.transpose(cache_sb_k, (0, 1, 3, 4, 2)).reshape(n_sb, bs, d, past)
    cache_vt = jnp.transpose(cache_sb_v, (0, 1, 3, 4, 2)).reshape(n_sb, bs, d, past)
    kv_prompt = None
    new_ks, new_vs = [], []
    new_h = {'p': [], 's': []}
    new_c = {'p': [], 's': []}

    for i in range(depth):
        j = i // 2
        g = norm_g[i].reshape(4, 1, d)
        wg1, wu1, wd1 = (w[i, 0].astype(BF16) for w in (ffn_w_gate, ffn_w_up, ffn_w_down))
        wg2, wu2, wd2 = (w[i, 1].astype(BF16) for w in (ffn_w_gate, ffn_w_up, ffn_w_down))
        wpe = ple_w_proj[i].astype(BF16)
        wpg = ple_w_gate[i].astype(BF16)
        if i % 2 == 0:
            wqkv = sb_w_qkv[j].astype(BF16)
            wq, wkt, wvt = wqkv[:, :d], wqkv[:, d:2 * d].T, wqkv[:, 2 * d:].T
            wo = sb_w_o[j].astype(BF16)
        else:
            w_in = lru_w_in[j].astype(BF16)
            w_gates = _gate_slabs(lru_w_a[j], lru_w_x[j])
            wo = lru_w_o[j].astype(BF16)
            vecs = [v[j].reshape(1, D_RNN) for v in (lru_conv_b, lru_b_a, lru_b_x, lru_lambda)]
        for grp in ('p', 's'):
            b, t = dims[grp]
            x = ffn1(xs[grp], g[0], wg1, wu1, wd1)
            if i % 2 == 0 and grp == 'p':
                q, kt, vt, ktb, vtb = qkv_proj_t(x.reshape(b, t, d), g[1], wq, wkt, wvt, j, n_sb, kv_prompt)
                kv_prompt = (kt, vt)
                a = sb_attention_prompt(q, ktb, vtb).reshape(b * t, d)
            elif i % 2 == 0:
                q, k, v, kb, vb = qkv_proj(x, g[1], wqkv)
                new_ks.append(k.reshape(b, t, SB_HEADS, SB_HEAD_DIM))
                new_vs.append(v.reshape(b, t, SB_HEADS, SB_HEAD_DIM))
                q3, kb3, vb3 = (arr.reshape(b, t, d) for arr in (q, kb, vb))
                a = sb_attention_sample(q3, kb3, vb3, cache_kt, cache_vt, j).reshape(b * t, d)
            else:
                if grp == 'p':
                    h0 = jnp.zeros((b, 1, D_RNN), F32)
                    c0 = jnp.zeros((b, CONV_W - 1, D_RNN), F32)
                else:
                    h0 = state_lru_h[j].reshape(b, 1, D_RNN)
                    c0 = state_lru_conv[j]
                a, hn, cn = lru_core(x.reshape(b, t, d), g[1], w_in, lru_conv_w[j], vecs[0], w_gates, vecs[1],
                                     vecs[2], vecs[3], h0, c0)
                new_h[grp].append(hn.reshape(b, D_RNN))
                new_c[grp].append(cn)
                a = a.reshape(b * t, D_RNN)
            xs[grp] = ffn2_ple(x, a, wo, ps[grp], i, g[2], wg2, wu2, wd2, g[3], wpe, wpg, gf,
                               final_norm=(i == depth - 1))

    def time_major(kvt):
        return jnp.transpose(kvt.reshape(n_sb, bp, SB_HEADS, SB_HEAD_DIM, tp), (0, 1, 4, 2, 3))

    return (xs['p'].reshape(bp, tp, d), xs['s'].reshape(bs, ts, d),
            time_major(kv_prompt[0]), time_major(kv_prompt[1]), jnp.stack(new_h['p']), jnp.stack(new_c['p']),
            jnp.stack(new_ks), jnp.stack(new_vs), jnp.stack(new_h['s']), jnp.stack(new_c['s']))
```

```python
import functools
import math

import jax
import jax.numpy as jnp
from jax import lax
from jax.experimental import pallas as pl
from jax.experimental.pallas import tpu as pltpu

D_MODEL = 1024
SB_HEADS = 16
SB_HEAD_DIM = 64
D_RNN = 1280
LRU_BLOCKS = 16
LRU_BLOCK = 80
CONV_W = 4
LRU_C = 8.0
D_FF = 2816
PLE_DIM = 256
EPS = 1e-6
LOG2E = 1.4426950408889634

BF16 = jnp.bfloat16
F32 = jnp.float32

V7X_VMEM_LIMIT = 56 * 1024 * 1024
V7X_MXU_COLS = 256


def _const_spec(shape):
    nd = len(shape)
    return pl.BlockSpec(shape, lambda *_: (0,) * nd, pipeline_mode=pl.Buffered(1))


def _rms(x, g):
    ms = jnp.mean(x * x, axis=-1, keepdims=True)
    return x * lax.rsqrt(ms + EPS) * g


def _dot(a, b):
    return jnp.dot(a, b, preferred_element_type=F32)


def _nt_dot(a, b):
    return lax.dot_general(a, b, (((1,), (1,)), ((), ())), preferred_element_type=F32)


def _tn_dot(a, b):
    return lax.dot_general(a, b, (((0,), (0,)), ((), ())), preferred_element_type=F32)


def _sigmoid(x):
    return 0.5 * jnp.tanh(0.5 * x) + 0.5


def _cparams(sem):
    return pltpu.CompilerParams(dimension_semantics=sem, vmem_limit_bytes=V7X_VMEM_LIMIT)


FF_CHUNK = 256


def _swiglu_half(x, g, wg_ref, wu_ref, wd_ref):
    xn = _rms(x, g).astype(BF16)
    acc = jnp.zeros(x.shape, F32)
    for c in range(D_FF // FF_CHUNK):
        sl = slice(c * FF_CHUNK, (c + 1) * FF_CHUNK)
        gate = _dot(xn, wg_ref[:, sl])
        up = _dot(xn, wu_ref[:, sl])
        h = (gate * _sigmoid(gate) * up).astype(BF16)
        acc = acc + _dot(h, wd_ref[sl, :])
    return x + 0.5 * acc


def _ffn1_kernel(x_ref, g_ref, wg_ref, wu_ref, wd_ref, o_ref):
    o_ref[...] = _swiglu_half(x_ref[...], g_ref[...], wg_ref, wu_ref, wd_ref)


def _ffn2_ple_kernel(x_ref, a_ref, wo_ref, p_ref, g2_ref, wg_ref, wu_ref, wd_ref, g3_ref, wpe_ref, wpg_ref,
                     gf_ref, o_ref, *, final_norm):
    x = x_ref[...] + _dot(a_ref[...], wo_ref[...])
    x = _swiglu_half(x, g2_ref[...], wg_ref, wu_ref, wd_ref)
    xn = _rms(x, g3_ref[...]).astype(BF16)
    gate = _sigmoid(_dot(xn, wpg_ref[...]))
    pe = _dot(p_ref[...].astype(BF16), wpe_ref[...])
    x = x + pe * gate
    if final_norm:
        x = _rms(x, gf_ref[...])
    o_ref[...] = x


def _token_tile(n):
    for tm in (512, 256):
        if n % tm == 0 and n // tm >= 4:
            return tm
    assert n % 128 == 0
    return 128


def _tok_spec(tm, width):
    return pl.BlockSpec((tm, width), lambda i: (i, 0))


def ffn1(x, g, wg, wu, wd):
    n = x.shape[0]
    tm = _token_tile(n)
    return pl.pallas_call(
        _ffn1_kernel,
        grid=(n // tm,),
        in_specs=[_tok_spec(tm, D_MODEL), _const_spec((1, D_MODEL)), _const_spec(wg.shape), _const_spec(wu.shape),
                  _const_spec(wd.shape)],
        out_specs=_tok_spec(tm, D_MODEL),
        out_shape=jax.ShapeDtypeStruct(x.shape, F32),
        compiler_params=_cparams(("parallel",)),
        name="ffn1",
    )(x, g, wg, wu, wd)


def ffn2_ple(x, a, wo, p_all, layer, g2, wg, wu, wd, g3, wpe, wpg, gf, final_norm):
    n = x.shape[0]
    tm = _token_tile(n)
    ka = a.shape[1]
    p_spec = pl.BlockSpec((None, tm, PLE_DIM), lambda i: (layer, i, 0))
    return pl.pallas_call(
        functools.partial(_ffn2_ple_kernel, final_norm=final_norm),
        grid=(n // tm,),
        in_specs=[_tok_spec(tm, D_MODEL), _tok_spec(tm, ka), _const_spec(wo.shape), p_spec,
                  _const_spec((1, D_MODEL)), _const_spec(wg.shape), _const_spec(wu.shape), _const_spec(wd.shape),
                  _const_spec((1, D_MODEL)), _const_spec(wpe.shape), _const_spec(wpg.shape),
                  _const_spec((1, D_MODEL))],
        out_specs=_tok_spec(tm, D_MODEL),
        out_shape=jax.ShapeDtypeStruct(x.shape, F32),
        compiler_params=_cparams(("parallel",)),
        name="ffn2_ple",
    )(x, a, wo, p_all, g2, wg, wu, wd, g3, wpe, wpg, gf)


Q_SCALE = -(SB_HEAD_DIM ** -0.5) * LOG2E


def _qkv_kernel(x_ref, g_ref, w_ref, q_ref, k_ref, v_ref, kb_ref, vb_ref):
    xn = _rms(x_ref[...], g_ref[...]).astype(BF16)
    d = D_MODEL
    q_ref[...] = (_dot(xn, w_ref[:, 0:d]) * Q_SCALE).astype(BF16)
    k = _dot(xn, w_ref[:, d:2 * d])
    k_ref[...] = k
    kb_ref[...] = k.astype(BF16)
    v = _dot(xn, w_ref[:, 2 * d:3 * d])
    v_ref[...] = v
    vb_ref[...] = v.astype(BF16)


def qkv_proj(x, g, w):
    n = x.shape[0]
    tm = _token_tile(n)
    tok = _tok_spec(tm, D_MODEL)
    return pl.pallas_call(
        _qkv_kernel,
        grid=(n // tm,),
        in_specs=[tok, _const_spec((1, D_MODEL)), _const_spec(w.shape)],
        out_specs=[tok] * 5,
        out_shape=[jax.ShapeDtypeStruct(x.shape, BF16), jax.ShapeDtypeStruct(x.shape, F32),
                   jax.ShapeDtypeStruct(x.shape, F32), jax.ShapeDtypeStruct(x.shape, BF16),
                   jax.ShapeDtypeStruct(x.shape, BF16)],
        compiler_params=_cparams(("parallel",)),
        name="qkv_proj",
    )(x, g, w)


def _qkv_t_kernel(x_ref, g_ref, wq_ref, wkt_ref, wvt_ref, *rest):
    q_ref, kt_ref, vt_ref, ktb_ref, vtb_ref = rest[-5:]
    xn = _rms(x_ref[0], g_ref[...]).astype(BF16)
    q_ref[0] = (_dot(xn, wq_ref[...]) * Q_SCALE).astype(BF16)
    kt = _nt_dot(wkt_ref[...], xn)
    kt_ref[0, 0] = kt
    ktb_ref[0] = kt.astype(BF16)
    vt = _nt_dot(wvt_ref[...], xn)
    vt_ref[0, 0] = vt
    vtb_ref[0] = vt.astype(BF16)


def qkv_proj_t(x, g, wq, wkt, wvt, slot, n_slots, prev=None):
    b, t, d = x.shape
    tm = _token_tile(t)
    tok = pl.BlockSpec((1, tm, d), lambda i, j: (i, j, 0))
    feat = pl.BlockSpec((1, d, tm), lambda i, j: (i, 0, j))
    slot_spec = pl.BlockSpec((1, 1, d, tm), lambda i, j: (slot, i, 0, j))
    in_specs = [tok, _const_spec((1, d)), _const_spec(wq.shape), _const_spec(wkt.shape), _const_spec(wvt.shape)]
    args = [x, g, wq, wkt, wvt]
    aliases = {}
    if prev is not None:
        in_specs += [pl.BlockSpec(memory_space=pl.ANY)] * 2
        aliases = {len(args): 1, len(args) + 1: 2}
        args += list(prev)
    slots = jax.ShapeDtypeStruct((n_slots, b, d, t), F32)
    return pl.pallas_call(
        _qkv_t_kernel,
        grid=(b, t // tm),
        in_specs=in_specs,
        out_specs=[tok, slot_spec, slot_spec, feat, feat],
        out_shape=[jax.ShapeDtypeStruct((b, t, d), BF16), slots, slots,
                   jax.ShapeDtypeStruct((b, d, t), BF16), jax.ShapeDtypeStruct((b, d, t), BF16)],
        input_output_aliases=aliases,
        compiler_params=_cparams(("parallel", "parallel")),
        name="qkv_proj_t",
    )(*args)


ATT_TK = 256
ATT_GROUP = 4


def _upper_ones(n):
    r = lax.broadcasted_iota(jnp.int32, (n, n), 0)
    c = lax.broadcasted_iota(jnp.int32, (n, n), 1)
    return jnp.where(c > r, 1.0, 0.0).astype(BF16)


def _log2_stay(nz):
    neg_abs = lax.bitcast_convert_type(lax.bitcast_convert_type(nz, jnp.int32) | jnp.int32(-2 ** 31), F32)
    return jnp.minimum(nz, 0.0) - jnp.log2(1.0 + jnp.exp2(neg_abs))


STICK_FLOOR = -160.0


def _sb_weights(nz, visibles, uppers, carries):
    n_t, n_g = len(nz), len(nz[0])
    stay = [[None] * n_g for _ in range(n_t)]
    after = [[None] * n_g for _ in range(n_t)]
    for t in range(n_t):
        for g in range(n_g):
            if nz[t][g] is None:
                continue
            s = _log2_stay(nz[t][g])
            if visibles[t][g] is not None:
                s = jnp.where(visibles[t][g], s, 0.0)
            stay[t][g] = s
            after[t][g] = _dot(uppers[t], s.astype(BF16))
    weights = [[None] * n_g for _ in range(n_t)]
    new_carries = []
    for g in range(n_g):
        carry = carries[g]
        for t in range(n_t):
            if nz[t][g] is None:
                continue
            w = jnp.exp2((stay[t][g] - nz[t][g]) + after[t][g] + carry)
            if visibles[t][g] is not None:
                w = jnp.where(visibles[t][g], w, 0.0)
            weights[t][g] = w.astype(BF16)
            carry = carry + after[t][g][0:1, :] + stay[t][g][0:1, :]
        new_carries.append(carry)
    return weights, new_carries


def _sb_prompt_kernel(q_ref, kt_ref, vt_ref, o_ref, k_ref, acc_ref, carry_ref, live_ref):
    n_tiles = q_ref.shape[1] // ATT_TK
    k_ref[...] = kt_ref[0].astype(F32).T.astype(BF16)
    upper = _upper_ones(ATT_TK)
    lane = lax.broadcasted_iota(jnp.int32, (1, 128), 1)
    head_a = lane < SB_HEAD_DIM
    kpos = lax.broadcasted_iota(jnp.int32, (ATT_TK, ATT_TK), 0)
    qpos = lax.broadcasted_iota(jnp.int32, (ATT_TK, ATT_TK), 1)
    causal = kpos < qpos
    row = lax.broadcasted_iota(jnp.int32, (128, 1), 0)

    def k_tile(i):
        return k_ref[pl.ds(pl.multiple_of(i * ATT_TK, ATT_TK), ATT_TK), :]

    def v_tile(i):
        return vt_ref[0, :, pl.ds(pl.multiple_of(i * ATT_TK, ATT_TK), ATT_TK)]

    def head_queries(i, n):
        q2 = q_ref[0, pl.ds(pl.multiple_of(i * ATT_TK, ATT_TK), n * ATT_TK), :]
        return [jnp.where(head_a, q2, jnp.zeros_like(q2)), jnp.where(head_a, jnp.zeros_like(q2), q2)]

    def stick_left(carries):
        return jnp.max(jnp.maximum(carries[0], carries[1])) > STICK_FLOOR

    def write_out(i):
        out_t = jnp.where(row < SB_HEAD_DIM, acc_ref[i, 0], acc_ref[i, 1])
        o_ref[0, pl.ds(pl.multiple_of(i * ATT_TK, ATT_TK), ATT_TK), :] = out_t.T.astype(BF16)

    def group(first, with_prev):
        m = ATT_GROUP
        heads = head_queries(first, m)
        own = [[None, None] for _ in range(m)]
        prev = [[None, None] for _ in range(m)]
        for i in range(-1 if with_prev else 0, m):
            lo, hi = max(i, 0), min(i + 1, m - 1)
            for h in range(2):
                nz = _nt_dot(k_tile(first + i), heads[h][lo * ATT_TK:(hi + 1) * ATT_TK])
                if i >= 0:
                    own[i][h] = nz[:, 0:ATT_TK]
                if i + 1 < m:
                    prev[i + 1][h] = nz[:, (i + 1 - lo) * ATT_TK:(i + 2 - lo) * ATT_TK]
        flat = lambda per_tile: [per_tile[j][h] for j in range(m) for h in range(2)]
        n_g = 2 * m
        zero = jnp.zeros((1, ATT_TK), F32)
        w, carries = _sb_weights([flat(own), flat(prev)], [[causal] * n_g, [None] * n_g], [upper, upper],
                                 [zero] * n_g)
        for j in range(m):
            for h in range(2):
                g = 2 * j + h
                out = _dot(v_tile(first + j), w[0][g])
                if w[1][g] is not None:
                    out = out + _dot(v_tile(first + j - 1), w[1][g])
                acc_ref[first + j, h] = out
                carry_ref[first + j, h] = jnp.broadcast_to(carries[g], (8, ATT_TK))
            live_ref[first + j] = stick_left(carries[2 * j:2 * j + 2]).astype(jnp.int32)
            write_out(first + j)

    group(0, False)

    def later_group(gi, _):
        group(gi * ATT_GROUP, True)
        return 0

    lax.fori_loop(1, n_tiles // ATT_GROUP, later_group, 0)

    def earlier_keys(i, _):
        def more_keys(state):
            n, live = state
            return jnp.logical_and(n < i - 1, live)

        def k_block(state):
            n, _ = state
            kt = i - 2 - n
            heads = head_queries(i, 1)
            nz = [[_nt_dot(k_tile(kt), qh) for qh in heads]]
            w, carries = _sb_weights(nz, [[None, None]], [upper], [carry_ref[i, 0, 0:1, :], carry_ref[i, 1, 0:1, :]])
            for h in range(2):
                acc_ref[i, h] += _dot(v_tile(kt), w[0][h])
                carry_ref[i, h] = jnp.broadcast_to(carries[h], (8, ATT_TK))
            return n + 1, stick_left(carries)

        n_done, _ = lax.while_loop(more_keys, k_block, (jnp.int32(0), live_ref[i] != 0))

        @pl.when(n_done > 0)
        def _():
            write_out(i)

        return 0

    lax.fori_loop(2, n_tiles, earlier_keys, 0)


def sb_attention_prompt(q, kt, vt):
    b, t, d = q.shape
    assert t % (ATT_GROUP * ATT_TK) == 0
    n_tiles = t // ATT_TK
    tok = pl.BlockSpec((1, t, 128), lambda i, j: (i, 0, j))
    feat = pl.BlockSpec((1, 128, t), lambda i, j: (i, j, 0))
    return pl.pallas_call(
        _sb_prompt_kernel,
        grid=(b, d // 128),
        in_specs=[tok, feat, feat],
        out_specs=tok,
        out_shape=jax.ShapeDtypeStruct(q.shape, BF16),
        scratch_shapes=[pltpu.VMEM((t, 128), BF16), pltpu.VMEM((n_tiles, 2, 128, ATT_TK), F32),
                        pltpu.VMEM((n_tiles, 2, 8, ATT_TK), F32), pltpu.SMEM((n_tiles,), jnp.int32)],
        compiler_params=_cparams(("parallel", "parallel")),
        name="sb_attn_prompt",
    )(q, kt, vt)


SAMPLE_HEADS = 4
SAMPLE_W = SAMPLE_HEADS * SB_HEAD_DIM


SAMPLE_GROUPS = 2


def _sb_sample_kernel(q_ref, kn_ref, vn_ref, ckt_ref, cvt_ref, o_ref, acc_ref, carry_ref):
    tq = q_ref.shape[1]
    past = ckt_ref.shape[3]
    lanes = SAMPLE_HEADS * tq
    n_grp = SAMPLE_GROUPS
    cols = [slice(g * SAMPLE_W, (g + 1) * SAMPLE_W) for g in range(n_grp)]
    upper = _upper_ones(ATT_TK)
    r_head = lax.broadcasted_iota(jnp.int32, (lanes, SAMPLE_W), 0) // tq
    c_head = lax.broadcasted_iota(jnp.int32, (lanes, SAMPLE_W), 1) // SB_HEAD_DIM
    own = r_head == c_head
    qm = []
    for g in range(n_grp):
        qrows = jnp.concatenate([q_ref[0, :, cols[g]]] * SAMPLE_HEADS, axis=0)
        qm.append(jnp.where(own, qrows, jnp.zeros_like(qrows)))

    def cache_keys(start, g):
        return ckt_ref[0, 0, cols[g], pl.ds(start, ATT_TK)].T.astype(BF16)

    def cache_values(start, g):
        return cvt_ref[0, 0, cols[g], pl.ds(start, ATT_TK)].astype(BF16)

    def stick_left(carries):
        m = carries[0]
        for c in carries[1:]:
            m = jnp.maximum(m, c)
        return jnp.max(m) > STICK_FLOOR

    kpos = lax.broadcasted_iota(jnp.int32, (tq, lanes), 0)
    qpos = lax.broadcasted_iota(jnp.int32, (tq, lanes), 1) % tq
    visible = kpos < qpos
    last = past - ATT_TK
    nz = [[_nt_dot(kn_ref[0, :, cols[g]], qm[g]) for g in range(n_grp)],
          [_nt_dot(cache_keys(last, g), qm[g]) for g in range(n_grp)]]
    w, carries = _sb_weights(nz, [[visible] * n_grp, [None] * n_grp], [upper[0:tq, 0:tq], upper],
                             [jnp.zeros((1, lanes), F32)] * n_grp)
    for g in range(n_grp):
        acc_ref[g] = _tn_dot(vn_ref[0, :, cols[g]], w[0][g]) + _dot(cache_values(last, g), w[1][g])
        carry_ref[g] = jnp.broadcast_to(carries[g], (8, lanes))

    def more_keys(state):
        n, live = state
        return jnp.logical_and(n < past // ATT_TK - 1, live)

    def k_block(state):
        n, _ = state
        start = pl.multiple_of(last - (n + 1) * ATT_TK, ATT_TK)
        nz = [[_nt_dot(cache_keys(start, g), qm[g]) for g in range(n_grp)]]
        w, carries = _sb_weights(nz, [[None] * n_grp], [upper], [carry_ref[g, 0:1, :] for g in range(n_grp)])
        for g in range(n_grp):
            acc_ref[g] += _dot(cache_values(start, g), w[0][g])
            carry_ref[g] = jnp.broadcast_to(carries[g], (8, lanes))
        return n + 1, stick_left(carries)

    lax.while_loop(more_keys, k_block, (jnp.int32(0), stick_left(carries)))

    for g in range(n_grp):
        acc_t = jnp.where(own, acc_ref[g].T, 0.0)
        out = acc_t[0:tq]
        for h in range(1, SAMPLE_HEADS):
            out = out + acc_t[h * tq:(h + 1) * tq]
        o_ref[0, :, cols[g]] = out.astype(BF16)


def sb_attention_sample(q, kn, vn, cache_kt, cache_vt, slot):
    b, tq, d = q.shape
    past = cache_kt.shape[3]
    width = SAMPLE_GROUPS * SAMPLE_W
    assert past % ATT_TK == 0 and d % width == 0
    new_spec = pl.BlockSpec((1, tq, width), lambda i, j: (i, 0, j))
    cache_spec = pl.BlockSpec((1, 1, width, past), lambda i, j: (slot, i, j, 0))
    lanes = SAMPLE_HEADS * tq
    return pl.pallas_call(
        _sb_sample_kernel,
        grid=(b, d // width),
        in_specs=[new_spec, new_spec, new_spec, cache_spec, cache_spec],
        out_specs=new_spec,
        out_shape=jax.ShapeDtypeStruct(q.shape, BF16),
        scratch_shapes=[pltpu.VMEM((SAMPLE_GROUPS, SAMPLE_W, lanes), F32),
                        pltpu.VMEM((SAMPLE_GROUPS, 8, lanes), F32)],
        compiler_params=_cparams(("parallel", "parallel")),
        name="sb_attn_sample",
    )(q, kn, vn, cache_kt, cache_vt)


TAIL_ROWS = 8
GATE_K = 512


def _gate_tile_starts():
    starts = []
    for n in range(D_RNN // V7X_MXU_COLS):
        first = (n * V7X_MXU_COLS) // LRU_BLOCK * LRU_BLOCK
        last = ((n + 1) * V7X_MXU_COLS - 1) // LRU_BLOCK * LRU_BLOCK + LRU_BLOCK
        lo = min(first // 128 * 128, D_RNN - GATE_K)
        assert lo <= first and last <= lo + GATE_K
        starts.append(lo)
    return starts


def _gate_slabs(w_a, w_x):
    def dense(w):
        eye = jnp.eye(LRU_BLOCKS, dtype=w.dtype)
        return jnp.einsum('njk,nm->njmk', w, eye).reshape(D_RNN, D_RNN)
    da, dx = dense(w_a), dense(w_x)
    slabs = []
    for n, lo in enumerate(_gate_tile_starts()):
        cols = slice(n * V7X_MXU_COLS, (n + 1) * V7X_MXU_COLS)
        slabs.append(jnp.concatenate([da[lo:lo + GATE_K, cols], dx[lo:lo + GATE_K, cols]], axis=1))
    return jnp.stack(slabs).astype(BF16)


def _gelu_tanh(x):
    c = math.sqrt(2.0 / math.pi)
    half_x = 0.5 * x
    return half_x + half_x * jnp.tanh(x * (c + (0.044715 * c) * (x * x)))


def _lru_kernel(x_ref, g_ref, win_ref, cw_ref, cb_ref, wg_ref, ba_ref, bx_ref, lam_ref, h0_ref, c0_ref,
                y_ref, hn_ref, cn_ref, ext_ref, a_ref, b_ref, h_ref, hstate_ref):
    tc = x_ref.shape[1]
    step = pl.program_id(1)

    @pl.when(step == 0)
    def _():
        hstate_ref[...] = h0_ref[0]
        ext_ref[0:TAIL_ROWS - (CONV_W - 1), :] = jnp.zeros((TAIL_ROWS - (CONV_W - 1), D_RNN), F32)
        ext_ref[TAIL_ROWS - (CONV_W - 1):TAIL_ROWS, :] = c0_ref[0]

    xn = _rms(x_ref[0], g_ref[...]).astype(BF16)
    gx = _dot(xn, win_ref[...])
    gate = _gelu_tanh(gx[:, :D_RNN])
    u = gx[:, D_RNN:]
    ext_ref[TAIL_ROWS:, :] = u
    c = cb_ref[...] + u * cw_ref[CONV_W - 1:CONV_W, :]
    for j in range(CONV_W - 1):
        shift = CONV_W - 1 - j
        c = c + ext_ref[TAIL_ROWS - shift:TAIL_ROWS - shift + tc, :] * cw_ref[j:j + 1, :]
    new_tail = ext_ref[tc + TAIL_ROWS - (CONV_W - 1):tc + TAIL_ROWS, :]
    ext_ref[TAIL_ROWS - (CONV_W - 1):TAIL_ROWS, :] = new_tail
    cn_ref[0] = new_tail

    cbf = c.astype(BF16)
    pre = [_dot(cbf[:, lo:lo + GATE_K], wg_ref[n]) for n, lo in enumerate(_gate_tile_starts())]
    r = _sigmoid(jnp.concatenate([p[:, :V7X_MXU_COLS] for p in pre], axis=-1) + ba_ref[...])
    gin = _sigmoid(jnp.concatenate([p[:, V7X_MXU_COLS:] for p in pre], axis=-1) + bx_ref[...])
    lam = lam_ref[...]
    softplus_neg_lam = jnp.maximum(-lam, 0.0) + jnp.log(1.0 + jnp.exp(-jnp.abs(lam)))
    log_a = (-LRU_C * softplus_neg_lam) * r
    a = jnp.exp(log_a)
    a_ref[...] = a
    one_minus_a2 = -jnp.tanh(log_a) * (a * a + 1.0)
    root = jnp.where(one_minus_a2 > 0.0, one_minus_a2 * lax.rsqrt(one_minus_a2), 0.0)
    b_ref[...] = root * (gin * c)

    def scan_step(i, h):
        h = a_ref[pl.ds(i, 1), :] * h + b_ref[pl.ds(i, 1), :]
        h_ref[pl.ds(i, 1), :] = h
        return h

    h_last = lax.fori_loop(0, tc, scan_step, hstate_ref[...], unroll=8)
    hstate_ref[...] = h_last
    hn_ref[0] = h_last
    y_ref[0] = (h_ref[...] * gate).astype(BF16)


def lru_core(x, g, w_in, conv_w, conv_b, w_gates, b_a, b_x, lam, h0, conv0):
    b, t, d = x.shape
    tc = 256 if t % 256 == 0 else t
    vec = _const_spec((1, D_RNN))
    return pl.pallas_call(
        _lru_kernel,
        grid=(b, t // tc),
        in_specs=[pl.BlockSpec((1, tc, d), lambda i, j: (i, j, 0)), _const_spec((1, D_MODEL)),
                  _const_spec(w_in.shape), _const_spec((CONV_W, D_RNN)), vec, _const_spec(w_gates.shape), vec,
                  vec, vec,
                  pl.BlockSpec((1, 1, D_RNN), lambda i, j: (i, 0, 0)),
                  pl.BlockSpec((1, CONV_W - 1, D_RNN), lambda i, j: (i, 0, 0))],
        out_specs=[pl.BlockSpec((1, tc, D_RNN), lambda i, j: (i, j, 0)),
                   pl.BlockSpec((1, 1, D_RNN), lambda i, j: (i, 0, 0)),
                   pl.BlockSpec((1, CONV_W - 1, D_RNN), lambda i, j: (i, 0, 0))],
        out_shape=[jax.ShapeDtypeStruct((b, t, D_RNN), BF16), jax.ShapeDtypeStruct((b, 1, D_RNN), F32),
                   jax.ShapeDtypeStruct((b, CONV_W - 1, D_RNN), F32)],
        scratch_shapes=[pltpu.VMEM((tc + TAIL_ROWS, D_RNN), F32), pltpu.VMEM((tc, D_RNN), F32),
                        pltpu.VMEM((tc, D_RNN), F32), pltpu.VMEM((tc, D_RNN), F32), pltpu.VMEM((1, D_RNN), F32)],
        compiler_params=_cparams(("parallel", "arbitrary")),
        name="lru_core",
    )(x, g, w_in, conv_w, conv_b, w_gates, b_a, b_x, lam, h0, conv0)


def kernel(x_prompt, x_sample, p_prompt, p_sample, cache_sb_k, cache_sb_v, state_lru_h, state_lru_conv, norm_g,
           ffn_w_gate, ffn_w_up, ffn_w_down, ple_w_proj, ple_w_gate, sb_w_qkv, sb_w_o, lru_w_in, lru_conv_w,
           lru_conv_b, lru_w_a, lru_b_a, lru_w_x, lru_b_x, lru_lambda, lru_w_o, final_norm_g):
    depth = norm_g.shape[0]
    bp, tp, d = x_prompt.shape
    bs, ts, _ = x_sample.shape
    n_sb, _, past = cache_sb_k.shape[:3]
    xs = {'p': x_prompt.reshape(bp * tp, d), 's': x_sample.reshape(bs * ts, d)}
    ps = {'p': p_prompt.reshape(depth, bp * tp, PLE_DIM), 's': p_sample.reshape(depth, bs * ts, PLE_DIM)}
    dims = {'p': (bp, tp), 's': (bs, ts)}
    gf = final_norm_g.reshape(1, d)
    cache_kt = jnp.transpose(cache_sb_k, (0, 1, 3, 4, 2)).reshape(n_sb, bs, d, past)
    cache_vt = jnp.transpose(cache_sb_v, (0, 1, 3, 4, 2)).reshape(n_sb, bs, d, past)
    kv_prompt = None
    new_ks, new_vs = [], []
    new_h = {'p': [], 's': []}
    new_c = {'p': [], 's': []}

    for i in range(depth):
        j = i // 2
        g = norm_g[i].reshape(4, 1, d)
        wg1, wu1, wd1 = (w[i, 0].astype(BF16) for w in (ffn_w_gate, ffn_w_up, ffn_w_down))
        wg2, wu2, wd2 = (w[i, 1].astype(BF16) for w in (ffn_w_gate, ffn_w_up, ffn_w_down))
        wpe = ple_w_proj[i].astype(BF16)
        wpg = ple_w_gate[i].astype(BF16)
        if i % 2 == 0:
            wqkv = sb_w_qkv[j].astype(BF16)
            wq, wkt, wvt = wqkv[:, :d], wqkv[:, d:2 * d].T, wqkv[:, 2 * d:].T
            wo = sb_w_o[j].astype(BF16)
        else:
            w_in = lru_w_in[j].astype(BF16)
            w_gates = _gate_slabs(lru_w_a[j], lru_w_x[j])
            wo = lru_w_o[j].astype(BF16)
            vecs = [v[j].reshape(1, D_RNN) for v in (lru_conv_b, lru_b_a, lru_b_x, lru_lambda)]
        for grp in ('p', 's'):
            b, t = dims[grp]
            x = ffn1(xs[grp], g[0], wg1, wu1, wd1)
            if i % 2 == 0 and grp == 'p':
                q, kt, vt, ktb, vtb = qkv_proj_t(x.reshape(b, t, d), g[1], wq, wkt, wvt, j, n_sb, kv_prompt)
                kv_prompt = (kt, vt)
                a = sb_attention_prompt(q, ktb, vtb).reshape(b * t, d)
            elif i % 2 == 0:
                q, k, v, kb, vb = qkv_proj(x, g[1], wqkv)
                new_ks.append(k.reshape(b, t, SB_HEADS, SB_HEAD_DIM))
                new_vs.append(v.reshape(b, t, SB_HEADS, SB_HEAD_DIM))
                q3, kb3, vb3 = (arr.reshape(b, t, d) for arr in (q, kb, vb))
                a = sb_attention_sample(q3, kb3, vb3, cache_kt, cache_vt, j).reshape(b * t, d)
            else:
                if grp == 'p':
                    h0 = jnp.zeros((b, 1, D_RNN), F32)
                    c0 = jnp.zeros((b, CONV_W - 1, D_RNN), F32)
                else:
                    h0 = state_lru_h[j].reshape(b, 1, D_RNN)
                    c0 = state_lru_conv[j]
                a, hn, cn = lru_core(x.reshape(b, t, d), g[1], w_in, lru_conv_w[j], vecs[0], w_gates, vecs[1],
                                     vecs[2], vecs[3], h0, c0)
                new_h[grp].append(hn.reshape(b, D_RNN))
                new_c[grp].append(cn)
                a = a.reshape(b * t, D_RNN)
            xs[grp] = ffn2_ple(x, a, wo, ps[grp], i, g[2], wg2, wu2, wd2, g[3], wpe, wpg, gf,
                               final_norm=(i == depth - 1))

    def time_major(kvt):
        return jnp.transpose(kvt.reshape(n_sb, bp, SB_HEADS, SB_HEAD_DIM, tp), (0, 1, 4, 2, 3))

    return (xs['p'].reshape(bp, tp, d), xs['s'].reshape(bs, ts, d),
            time_major(kv_prompt[0]), time_major(kv_prompt[1]), jnp.stack(new_h['p']), jnp.stack(new_c['p']),
            jnp.stack(new_ks), jnp.stack(new_vs), jnp.stack(new_h['s']), jnp.stack(new_c['s']))
```

```python
import functools
import math

import jax
import jax.numpy as jnp
from jax import lax
from jax.experimental import pallas as pl
from jax.experimental.pallas import tpu as pltpu

D_MODEL = 1024
SB_HEADS = 16
SB_HEAD_DIM = 64
D_RNN = 1280
LRU_BLOCKS = 16
LRU_BLOCK = 80
CONV_W = 4
LRU_C = 8.0
D_FF = 2816
PLE_DIM = 256
EPS = 1e-6
LOG2E = 1.4426950408889634

BF16 = jnp.bfloat16
F32 = jnp.float32

V7X_VMEM_LIMIT = 56 * 1024 * 1024
V7X_MXU_COLS = 256


def _const_spec(shape):
    nd = len(shape)
    return pl.BlockSpec(shape, lambda *_: (0,) * nd, pipeline_mode=pl.Buffered(1))


def _rms(x, g):
    ms = jnp.mean(x * x, axis=-1, keepdims=True)
    return x * lax.rsqrt(ms + EPS) * g


def _dot(a, b):
    return jnp.dot(a, b, preferred_element_type=F32)


def _nt_dot(a, b):
    return lax.dot_general(a, b, (((1,), (1,)), ((), ())), preferred_element_type=F32)


def _tn_dot(a, b):
    return lax.dot_general(a, b, (((0,), (0,)), ((), ())), preferred_element_type=F32)


def _sigmoid(x):
    return 0.5 * jnp.tanh(0.5 * x) + 0.5


def _cparams(sem):
    return pltpu.CompilerParams(dimension_semantics=sem, vmem_limit_bytes=V7X_VMEM_LIMIT)


FF_CHUNK = 256


def _swiglu_half(x, g, wg_ref, wu_ref, wd_ref):
    xn = _rms(x, g).astype(BF16)
    acc = jnp.zeros(x.shape, F32)
    for c in range(D_FF // FF_CHUNK):
        sl = slice(c * FF_CHUNK, (c + 1) * FF_CHUNK)
        gate = _dot(xn, wg_ref[:, sl])
        up = _dot(xn, wu_ref[:, sl])
        h = (gate * _sigmoid(gate) * up).astype(BF16)
        acc = acc + _dot(h, wd_ref[sl, :])
    return x + 0.5 * acc


def _ffn1_kernel(x_ref, g_ref, wg_ref, wu_ref, wd_ref, o_ref):
    o_ref[...] = _swiglu_half(x_ref[...], g_ref[...], wg_ref, wu_ref, wd_ref)


def _ffn2_ple_kernel(x_ref, a_ref, wo_ref, p_ref, g2_ref, wg_ref, wu_ref, wd_ref, g3_ref, wpe_ref, wpg_ref,
                     gf_ref, o_ref, *, final_norm):
    x = x_ref[...] + _dot(a_ref[...], wo_ref[...])
    x = _swiglu_half(x, g2_ref[...], wg_ref, wu_ref, wd_ref)
    xn = _rms(x, g3_ref[...]).astype(BF16)
    gate = _sigmoid(_dot(xn, wpg_ref[...]))
    pe = _dot(p_ref[...].astype(BF16), wpe_ref[...])
    x = x + pe * gate
    if final_norm:
        x = _rms(x, gf_ref[...])
    o_ref[...] = x


def _token_tile(n):
    if n % 512 == 0 and n // 512 >= 4:
        return 512
    assert n % 256 == 0
    return n // 2


def _tok_spec(tm, width):
    return pl.BlockSpec((tm, width), lambda i: (i, 0))


def ffn1(x, g, wg, wu, wd):
    n = x.shape[0]
    tm = _token_tile(n)
    return pl.pallas_call(
        _ffn1_kernel,
        grid=(n // tm,),
        in_specs=[_tok_spec(tm, D_MODEL), _const_spec((1, D_MODEL)), _const_spec(wg.shape), _const_spec(wu.shape),
                  _const_spec(wd.shape)],
        out_specs=_tok_spec(tm, D_MODEL),
        out_shape=jax.ShapeDtypeStruct(x.shape, F32),
        compiler_params=_cparams(("parallel",)),
        name="ffn1",
    )(x, g, wg, wu, wd)


def ffn2_ple(x, a, wo, p_all, layer, g2, wg, wu, wd, g3, wpe, wpg, gf, final_norm):
    n = x.shape[0]
    tm = _token_tile(n)
    ka = a.shape[1]
    p_spec = pl.BlockSpec((None, tm, PLE_DIM), lambda i: (layer, i, 0))
    return pl.pallas_call(
        functools.partial(_ffn2_ple_kernel, final_norm=final_norm),
        grid=(n // tm,),
        in_specs=[_tok_spec(tm, D_MODEL), _tok_spec(tm, ka), _const_spec(wo.shape), p_spec,
                  _const_spec((1, D_MODEL)), _const_spec(wg.shape), _const_spec(wu.shape), _const_spec(wd.shape),
                  _const_spec((1, D_MODEL)), _const_spec(wpe.shape), _const_spec(wpg.shape),
                  _const_spec((1, D_MODEL))],
        out_specs=_tok_spec(tm, D_MODEL),
        out_shape=jax.ShapeDtypeStruct(x.shape, F32),
        compiler_params=_cparams(("parallel",)),
        name="ffn2_ple",
    )(x, a, wo, p_all, g2, wg, wu, wd, g3, wpe, wpg, gf)


Q_SCALE = -(SB_HEAD_DIM ** -0.5) * LOG2E


def _qkv_kernel(x_ref, g_ref, w_ref, q_ref, k_ref, v_ref, kb_ref, vb_ref):
    xn = _rms(x_ref[...], g_ref[...]).astype(BF16)
    d = D_MODEL
    q_ref[...] = (_dot(xn, w_ref[:, 0:d]) * Q_SCALE).astype(BF16)
    k = _dot(xn, w_ref[:, d:2 * d])
    k_ref[...] = k
    kb_ref[...] = k.astype(BF16)
    v = _dot(xn, w_ref[:, 2 * d:3 * d])
    v_ref[...] = v
    vb_ref[...] = v.astype(BF16)


def qkv_proj(x, g, w):
    n = x.shape[0]
    tm = _token_tile(n)
    tok = _tok_spec(tm, D_MODEL)
    return pl.pallas_call(
        _qkv_kernel,
        grid=(n // tm,),
        in_specs=[tok, _const_spec((1, D_MODEL)), _const_spec(w.shape)],
        out_specs=[tok] * 5,
        out_shape=[jax.ShapeDtypeStruct(x.shape, BF16), jax.ShapeDtypeStruct(x.shape, F32),
                   jax.ShapeDtypeStruct(x.shape, F32), jax.ShapeDtypeStruct(x.shape, BF16),
                   jax.ShapeDtypeStruct(x.shape, BF16)],
        compiler_params=_cparams(("parallel",)),
        name="qkv_proj",
    )(x, g, w)


def _qkv_t_kernel(x_ref, g_ref, wq_ref, wkt_ref, wvt_ref, *rest):
    q_ref, kt_ref, vt_ref, kb_ref, vtb_ref = rest[-5:]
    xn = _rms(x_ref[0], g_ref[...]).astype(BF16)
    q_ref[0] = (_dot(xn, wq_ref[...]) * Q_SCALE).astype(BF16)
    kt = _nt_dot(wkt_ref[...], xn)
    kt_ref[0, 0] = kt
    kb_ref[0] = kt.T.astype(BF16)
    vt = _nt_dot(wvt_ref[...], xn)
    vt_ref[0, 0] = vt
    vtb_ref[0] = vt.astype(BF16)


def qkv_proj_t(x, g, wq, wkt, wvt, slot, n_slots, prev=None):
    b, t, d = x.shape
    tm = _token_tile(t)
    tok = pl.BlockSpec((1, tm, d), lambda i, j: (i, j, 0))
    feat = pl.BlockSpec((1, d, tm), lambda i, j: (i, 0, j))
    slot_spec = pl.BlockSpec((1, 1, d, tm), lambda i, j: (slot, i, 0, j))
    in_specs = [tok, _const_spec((1, d)), _const_spec(wq.shape), _const_spec(wkt.shape), _const_spec(wvt.shape)]
    args = [x, g, wq, wkt, wvt]
    aliases = {}
    if prev is not None:
        in_specs += [pl.BlockSpec(memory_space=pl.ANY)] * 2
        aliases = {len(args): 1, len(args) + 1: 2}
        args += list(prev)
    slots = jax.ShapeDtypeStruct((n_slots, b, d, t), F32)
    return pl.pallas_call(
        _qkv_t_kernel,
        grid=(b, t // tm),
        in_specs=in_specs,
        out_specs=[tok, slot_spec, slot_spec, tok, feat],
        out_shape=[jax.ShapeDtypeStruct((b, t, d), BF16), slots, slots,
                   jax.ShapeDtypeStruct((b, t, d), BF16), jax.ShapeDtypeStruct((b, d, t), BF16)],
        input_output_aliases=aliases,
        compiler_params=_cparams(("parallel", "parallel")),
        name="qkv_proj_t",
    )(*args)


ATT_TK = 256
ATT_GROUP = 4


def _upper_ones(n):
    r = lax.broadcasted_iota(jnp.int32, (n, n), 0)
    c = lax.broadcasted_iota(jnp.int32, (n, n), 1)
    return jnp.where(c > r, 1.0, 0.0).astype(BF16)


def _log2_stay(nz):
    neg_abs = lax.bitcast_convert_type(lax.bitcast_convert_type(nz, jnp.int32) | jnp.int32(-2 ** 31), F32)
    return jnp.minimum(nz, 0.0) - jnp.log2(1.0 + jnp.exp2(neg_abs))


STICK_FLOOR = -160.0


def _sb_weights(nz, visibles, uppers, carries):
    n_t, n_g = len(nz), len(nz[0])
    stay = [[None] * n_g for _ in range(n_t)]
    after = [[None] * n_g for _ in range(n_t)]
    for t in range(n_t):
        for g in range(n_g):
            if nz[t][g] is None:
                continue
            s = _log2_stay(nz[t][g])
            if visibles[t][g] is not None:
                s = jnp.where(visibles[t][g], s, 0.0)
            stay[t][g] = s
            after[t][g] = _dot(uppers[t], s.astype(BF16))
    weights = [[None] * n_g for _ in range(n_t)]
    new_carries = []
    for g in range(n_g):
        carry = carries[g]
        for t in range(n_t):
            if nz[t][g] is None:
                continue
            w = jnp.exp2((stay[t][g] - nz[t][g]) + after[t][g] + carry)
            if visibles[t][g] is not None:
                w = jnp.where(visibles[t][g], w, 0.0)
            weights[t][g] = w.astype(BF16)
            carry = carry + after[t][g][0:1, :] + stay[t][g][0:1, :]
        new_carries.append(carry)
    return weights, new_carries


def _sb_prompt_kernel(q_ref, k_ref, vt_ref, o_ref, acc_ref, carry_ref, live_ref):
    n_tiles = q_ref.shape[1] // ATT_TK
    upper = _upper_ones(ATT_TK)
    lane = lax.broadcasted_iota(jnp.int32, (1, 128), 1)
    head_a = lane < SB_HEAD_DIM
    kpos = lax.broadcasted_iota(jnp.int32, (ATT_TK, ATT_TK), 0)
    qpos = lax.broadcasted_iota(jnp.int32, (ATT_TK, ATT_TK), 1)
    causal = kpos < qpos
    row = lax.broadcasted_iota(jnp.int32, (128, 1), 0)

    def k_tile(i):
        return k_ref[0, pl.ds(pl.multiple_of(i * ATT_TK, ATT_TK), ATT_TK), :]

    def v_tile(i):
        return vt_ref[0, :, pl.ds(pl.multiple_of(i * ATT_TK, ATT_TK), ATT_TK)]

    def head_queries(i, n):
        q2 = q_ref[0, pl.ds(pl.multiple_of(i * ATT_TK, ATT_TK), n * ATT_TK), :]
        return [jnp.where(head_a, q2, jnp.zeros_like(q2)), jnp.where(head_a, jnp.zeros_like(q2), q2)]

    def stick_left(carries):
        return jnp.max(jnp.maximum(carries[0], carries[1])) > STICK_FLOOR

    def write_out(i):
        out_t = jnp.where(row < SB_HEAD_DIM, acc_ref[i, 0], acc_ref[i, 1])
        o_ref[0, pl.ds(pl.multiple_of(i * ATT_TK, ATT_TK), ATT_TK), :] = out_t.T.astype(BF16)

    def group(first, with_prev):
        m = ATT_GROUP
        heads = head_queries(first, m)
        own = [[None, None] for _ in range(m)]
        prev = [[None, None] for _ in range(m)]
        for i in range(-1 if with_prev else 0, m):
            lo, hi = max(i, 0), min(i + 1, m - 1)
            for h in range(2):
                nz = _nt_dot(k_tile(first + i), heads[h][lo * ATT_TK:(hi + 1) * ATT_TK])
                if i >= 0:
                    own[i][h] = nz[:, 0:ATT_TK]
                if i + 1 < m:
                    prev[i + 1][h] = nz[:, (i + 1 - lo) * ATT_TK:(i + 2 - lo) * ATT_TK]
        flat = lambda per_tile: [per_tile[j][h] for j in range(m) for h in range(2)]
        n_g = 2 * m
        zero = jnp.zeros((1, ATT_TK), F32)
        w, carries = _sb_weights([flat(own), flat(prev)], [[causal] * n_g, [None] * n_g], [upper, upper],
                                 [zero] * n_g)
        for j in range(m):
            for h in range(2):
                g = 2 * j + h
                out = _dot(v_tile(first + j), w[0][g])
                if w[1][g] is not None:
                    out = out + _dot(v_tile(first + j - 1), w[1][g])
                acc_ref[first + j, h] = out
                carry_ref[first + j, h] = jnp.broadcast_to(carries[g], (8, ATT_TK))
            live_ref[first + j] = stick_left(carries[2 * j:2 * j + 2]).astype(jnp.int32)
            write_out(first + j)

    group(0, False)

    def later_group(gi, _):
        group(gi * ATT_GROUP, True)
        return 0

    lax.fori_loop(1, n_tiles // ATT_GROUP, later_group, 0)

    def earlier_keys(i, _):
        def more_keys(state):
            n, live = state
            return jnp.logical_and(n < i - 1, live)

        def k_block(state):
            n, _ = state
            kt = i - 2 - n
            heads = head_queries(i, 1)
            nz = [[_nt_dot(k_tile(kt), qh) for qh in heads]]
            w, carries = _sb_weights(nz, [[None, None]], [upper], [carry_ref[i, 0, 0:1, :], carry_ref[i, 1, 0:1, :]])
            for h in range(2):
                acc_ref[i, h] += _dot(v_tile(kt), w[0][h])
                carry_ref[i, h] = jnp.broadcast_to(carries[h], (8, ATT_TK))
            return n + 1, stick_left(carries)

        n_done, _ = lax.while_loop(more_keys, k_block, (jnp.int32(0), live_ref[i] != 0))

        @pl.when(n_done > 0)
        def _():
            write_out(i)

        return 0

    lax.fori_loop(2, n_tiles, earlier_keys, 0)


def sb_attention_prompt(q, k, vt):
    b, t, d = q.shape
    assert t % (ATT_GROUP * ATT_TK) == 0
    n_tiles = t // ATT_TK
    tok = pl.BlockSpec((1, t, 128), lambda i, j: (i, 0, j))
    feat = pl.BlockSpec((1, 128, t), lambda i, j: (i, j, 0))
    return pl.pallas_call(
        _sb_prompt_kernel,
        grid=(b, d // 128),
        in_specs=[tok, tok, feat],
        out_specs=tok,
        out_shape=jax.ShapeDtypeStruct(q.shape, BF16),
        scratch_shapes=[pltpu.VMEM((n_tiles, 2, 128, ATT_TK), F32), pltpu.VMEM((n_tiles, 2, 8, ATT_TK), F32),
                        pltpu.SMEM((n_tiles,), jnp.int32)],
        compiler_params=_cparams(("parallel", "parallel")),
        name="sb_attn_prompt",
    )(q, k, vt)


SAMPLE_HEADS = 4
SAMPLE_W = SAMPLE_HEADS * SB_HEAD_DIM


SAMPLE_GROUPS = 2


def _sb_sample_kernel(q_ref, kn_ref, vn_ref, ckt_ref, cvt_ref, o_ref, acc_ref, carry_ref):
    tq = q_ref.shape[1]
    past = ckt_ref.shape[3]
    lanes = SAMPLE_HEADS * tq
    n_grp = SAMPLE_GROUPS
    cols = [slice(g * SAMPLE_W, (g + 1) * SAMPLE_W) for g in range(n_grp)]
    upper = _upper_ones(ATT_TK)
    r_head = lax.broadcasted_iota(jnp.int32, (lanes, SAMPLE_W), 0) // tq
    c_head = lax.broadcasted_iota(jnp.int32, (lanes, SAMPLE_W), 1) // SB_HEAD_DIM
    own = r_head == c_head
    qm = []
    for g in range(n_grp):
        qrows = jnp.concatenate([q_ref[0, :, cols[g]]] * SAMPLE_HEADS, axis=0)
        qm.append(jnp.where(own, qrows, jnp.zeros_like(qrows)))

    def cache_keys(start, g):
        return ckt_ref[0, 0, cols[g], pl.ds(start, ATT_TK)].T.astype(BF16)

    def cache_values(start, g):
        return cvt_ref[0, 0, cols[g], pl.ds(start, ATT_TK)].astype(BF16)

    def stick_left(carries):
        m = carries[0]
        for c in carries[1:]:
            m = jnp.maximum(m, c)
        return jnp.max(m) > STICK_FLOOR

    kpos = lax.broadcasted_iota(jnp.int32, (tq, lanes), 0)
    qpos = lax.broadcasted_iota(jnp.int32, (tq, lanes), 1) % tq
    visible = kpos < qpos
    last = past - ATT_TK
    nz = [[_nt_dot(kn_ref[0, :, cols[g]], qm[g]) for g in range(n_grp)],
          [_nt_dot(cache_keys(last, g), qm[g]) for g in range(n_grp)]]
    w, carries = _sb_weights(nz, [[visible] * n_grp, [None] * n_grp], [upper[0:tq, 0:tq], upper],
                             [jnp.zeros((1, lanes), F32)] * n_grp)
    for g in range(n_grp):
        acc_ref[g] = _tn_dot(vn_ref[0, :, cols[g]], w[0][g]) + _dot(cache_values(last, g), w[1][g])
        carry_ref[g] = jnp.broadcast_to(carries[g], (8, lanes))

    def more_keys(state):
        n, live = state
        return jnp.logical_and(n < past // ATT_TK - 1, live)

    def k_block(state):
        n, _ = state
        start = pl.multiple_of(last - (n + 1) * ATT_TK, ATT_TK)
        nz = [[_nt_dot(cache_keys(start, g), qm[g]) for g in range(n_grp)]]
        w, carries = _sb_weights(nz, [[None] * n_grp], [upper], [carry_ref[g, 0:1, :] for g in range(n_grp)])
        for g in range(n_grp):
            acc_ref[g] += _dot(cache_values(start, g), w[0][g])
            carry_ref[g] = jnp.broadcast_to(carries[g], (8, lanes))
        return n + 1, stick_left(carries)

    lax.while_loop(more_keys, k_block, (jnp.int32(0), stick_left(carries)))

    for g in range(n_grp):
        acc_t = jnp.where(own, acc_ref[g].T, 0.0)
        out = acc_t[0:tq]
        for h in range(1, SAMPLE_HEADS):
            out = out + acc_t[h * tq:(h + 1) * tq]
        o_ref[0, :, cols[g]] = out.astype(BF16)


def sb_attention_sample(q, kn, vn, cache_kt, cache_vt, slot):
    b, tq, d = q.shape
    past = cache_kt.shape[3]
    width = SAMPLE_GROUPS * SAMPLE_W
    assert past % ATT_TK == 0 and d % width == 0
    new_spec = pl.BlockSpec((1, tq, width), lambda i, j: (i, 0, j))
    cache_spec = pl.BlockSpec((1, 1, width, past), lambda i, j: (slot, i, j, 0))
    lanes = SAMPLE_HEADS * tq
    return pl.pallas_call(
        _sb_sample_kernel,
        grid=(b, d // width),
        in_specs=[new_spec, new_spec, new_spec, cache_spec, cache_spec],
        out_specs=new_spec,
        out_shape=jax.ShapeDtypeStruct(q.shape, BF16),
        scratch_shapes=[pltpu.VMEM((SAMPLE_GROUPS, SAMPLE_W, lanes), F32),
                        pltpu.VMEM((SAMPLE_GROUPS, 8, lanes), F32)],
        compiler_params=_cparams(("parallel", "parallel")),
        name="sb_attn_sample",
    )(q, kn, vn, cache_kt, cache_vt)


TAIL_ROWS = 8
GATE_K = 512
LRU_SUB = 256
LRU_SUBS_PER_STEP = 2


def _gate_tile_starts():
    starts = []
    for n in range(D_RNN // V7X_MXU_COLS):
        first = (n * V7X_MXU_COLS) // LRU_BLOCK * LRU_BLOCK
        last = ((n + 1) * V7X_MXU_COLS - 1) // LRU_BLOCK * LRU_BLOCK + LRU_BLOCK
        lo = min(first // 128 * 128, D_RNN - GATE_K)
        assert lo <= first and last <= lo + GATE_K
        starts.append(lo)
    return starts


def _gate_slabs(w_a, w_x):
    def dense(w):
        eye = jnp.eye(LRU_BLOCKS, dtype=w.dtype)
        return jnp.einsum('njk,nm->njmk', w, eye).reshape(D_RNN, D_RNN)
    da, dx = dense(w_a), dense(w_x)
    slabs = []
    for n, lo in enumerate(_gate_tile_starts()):
        cols = slice(n * V7X_MXU_COLS, (n + 1) * V7X_MXU_COLS)
        slabs.append(jnp.concatenate([da[lo:lo + GATE_K, cols], dx[lo:lo + GATE_K, cols]], axis=1))
    return jnp.stack(slabs).astype(BF16)


def _gelu_tanh(x):
    c = math.sqrt(2.0 / math.pi)
    half_x = 0.5 * x
    return half_x + half_x * jnp.tanh(x * (c + (0.044715 * c) * (x * x)))


def _lru_kernel(x_ref, g_ref, win_ref, cw_ref, cb_ref, wg_ref, ba_ref, bx_ref, lam_ref, h0_ref, c0_ref,
                y_ref, hn_ref, cn_ref, ext_ref, a_ref, b_ref, h_ref, hstate_ref):
    tc = x_ref.shape[1]
    n_sub, sub = a_ref.shape[0], a_ref.shape[1]
    step = pl.program_id(1)

    @pl.when(step == 0)
    def _():
        hstate_ref[...] = h0_ref[0]
        ext_ref[0:TAIL_ROWS - (CONV_W - 1), :] = jnp.zeros((TAIL_ROWS - (CONV_W - 1), D_RNN), F32)
        ext_ref[TAIL_ROWS - (CONV_W - 1):TAIL_ROWS, :] = c0_ref[0]

    lam = lam_ref[...]
    softplus_neg_lam = jnp.maximum(-lam, 0.0) + jnp.log(1.0 + jnp.exp(-jnp.abs(lam)))
    decay = -LRU_C * softplus_neg_lam
    h = hstate_ref[...]
    for s in range(n_sub):
        r0 = s * sub
        xn = _rms(x_ref[0, r0:r0 + sub, :], g_ref[...]).astype(BF16)
        gx = _dot(xn, win_ref[...])
        gate = _gelu_tanh(gx[:, :D_RNN])
        u = gx[:, D_RNN:]
        ext_ref[TAIL_ROWS + r0:TAIL_ROWS + r0 + sub, :] = u
        c = cb_ref[...] + u * cw_ref[CONV_W - 1:CONV_W, :]
        for j in range(CONV_W - 1):
            first = TAIL_ROWS + r0 - (CONV_W - 1 - j)
            c = c + ext_ref[first:first + sub, :] * cw_ref[j:j + 1, :]
        cbf = c.astype(BF16)
        pre = [_dot(cbf[:, lo:lo + GATE_K], wg_ref[n]) for n, lo in enumerate(_gate_tile_starts())]
        r = _sigmoid(jnp.concatenate([p[:, :V7X_MXU_COLS] for p in pre], axis=-1) + ba_ref[...])
        gin = _sigmoid(jnp.concatenate([p[:, V7X_MXU_COLS:] for p in pre], axis=-1) + bx_ref[...])
        log_a = decay * r
        a = jnp.exp(log_a)
        a_ref[s] = a
        one_minus_a2 = -jnp.tanh(log_a) * (a * a + 1.0)
        root = jnp.where(one_minus_a2 > 0.0, one_minus_a2 * lax.rsqrt(one_minus_a2), 0.0)
        b_ref[s] = root * (gin * c)
        for i in range(sub):
            h = a_ref[s, i:i + 1, :] * h + b_ref[s, i:i + 1, :]
            h_ref[s, i:i + 1, :] = h
        y_ref[0, r0:r0 + sub, :] = (h_ref[s] * gate).astype(BF16)

    new_tail = ext_ref[tc + TAIL_ROWS - (CONV_W - 1):tc + TAIL_ROWS, :]
    ext_ref[TAIL_ROWS - (CONV_W - 1):TAIL_ROWS, :] = new_tail
    cn_ref[0] = new_tail
    hstate_ref[...] = h
    hn_ref[0] = h


def lru_core(x, g, w_in, conv_w, conv_b, w_gates, b_a, b_x, lam, h0, conv0):
    b, t, d = x.shape
    sub = min(LRU_SUB, t)
    tc = min(LRU_SUBS_PER_STEP * sub, t)
    assert t % tc == 0 and tc % sub == 0
    vec = _const_spec((1, D_RNN))
    return pl.pallas_call(
        _lru_kernel,
        grid=(b, t // tc),
        in_specs=[pl.BlockSpec((1, tc, d), lambda i, j: (i, j, 0)), _const_spec((1, D_MODEL)),
                  _const_spec(w_in.shape), _const_spec((CONV_W, D_RNN)), vec, _const_spec(w_gates.shape), vec,
                  vec, vec,
                  pl.BlockSpec((1, 1, D_RNN), lambda i, j: (i, 0, 0)),
                  pl.BlockSpec((1, CONV_W - 1, D_RNN), lambda i, j: (i, 0, 0))],
        out_specs=[pl.BlockSpec((1, tc, D_RNN), lambda i, j: (i, j, 0)),
                   pl.BlockSpec((1, 1, D_RNN), lambda i, j: (i, 0, 0)),
                   pl.BlockSpec((1, CONV_W - 1, D_RNN), lambda i, j: (i, 0, 0))],
        out_shape=[jax.ShapeDtypeStruct((b, t, D_RNN), BF16), jax.ShapeDtypeStruct((b, 1, D_RNN), F32),
                   jax.ShapeDtypeStruct((b, CONV_W - 1, D_RNN), F32)],
        scratch_shapes=[pltpu.VMEM((tc + TAIL_ROWS, D_RNN), F32)]
        + [pltpu.VMEM((tc // sub, sub, D_RNN), F32)] * 3 + [pltpu.VMEM((1, D_RNN), F32)],
        compiler_params=_cparams(("parallel", "arbitrary")),
        name="lru_core",
    )(x, g, w_in, conv_w, conv_b, w_gates, b_a, b_x, lam, h0, conv0)


def kernel(x_prompt, x_sample, p_prompt, p_sample, cache_sb_k, cache_sb_v, state_lru_h, state_lru_conv, norm_g,
           ffn_w_gate, ffn_w_up, ffn_w_down, ple_w_proj, ple_w_gate, sb_w_qkv, sb_w_o, lru_w_in, lru_conv_w,
           lru_conv_b, lru_w_a, lru_b_a, lru_w_x, lru_b_x, lru_lambda, lru_w_o, final_norm_g):
    depth = norm_g.shape[0]
    bp, tp, d = x_prompt.shape
    bs, ts, _ = x_sample.shape
    n_sb, _, past = cache_sb_k.shape[:3]
    xs = {'p': x_prompt.reshape(bp * tp, d), 's': x_sample.reshape(bs * ts, d)}
    ps = {'p': p_prompt.reshape(depth, bp * tp, PLE_DIM), 's': p_sample.reshape(depth, bs * ts, PLE_DIM)}
    dims = {'p': (bp, tp), 's': (bs, ts)}
    gf = final_norm_g.reshape(1, d)
    cache_kt = jnp.transpose(cache_sb_k, (0, 1, 3, 4, 2)).reshape(n_sb, bs, d, past)
    cache_vt = jnp.transpose(cache_sb_v, (0, 1, 3, 4, 2)).reshape(n_sb, bs, d, past)
    kv_prompt = None
    new_ks, new_vs = [], []
    new_h = {'p': [], 's': []}
    new_c = {'p': [], 's': []}

    for i in range(depth):
        j = i // 2
        g = norm_g[i].reshape(4, 1, d)
        wg1, wu1, wd1 = (w[i, 0].astype(BF16) for w in (ffn_w_gate, ffn_w_up, ffn_w_down))
        wg2, wu2, wd2 = (w[i, 1].astype(BF16) for w in (ffn_w_gate, ffn_w_up, ffn_w_down))
        wpe = ple_w_proj[i].astype(BF16)
        wpg = ple_w_gate[i].astype(BF16)
        if i % 2 == 0:
            wqkv = sb_w_qkv[j].astype(BF16)
            wq, wkt, wvt = wqkv[:, :d], wqkv[:, d:2 * d].T, wqkv[:, 2 * d:].T
            wo = sb_w_o[j].astype(BF16)
        else:
            w_in = lru_w_in[j].astype(BF16)
            w_gates = _gate_slabs(lru_w_a[j], lru_w_x[j])
            wo = lru_w_o[j].astype(BF16)
            vecs = [v[j].reshape(1, D_RNN) for v in (lru_conv_b, lru_b_a, lru_b_x, lru_lambda)]
        for grp in ('p', 's'):
            b, t = dims[grp]
            x = ffn1(xs[grp], g[0], wg1, wu1, wd1)
            if i % 2 == 0 and grp == 'p':
                q, kt, vt, kb, vtb = qkv_proj_t(x.reshape(b, t, d), g[1], wq, wkt, wvt, j, n_sb, kv_prompt)
                kv_prompt = (kt, vt)
                a = sb_attention_prompt(q, kb, vtb).reshape(b * t, d)
            elif i % 2 == 0:
                q, k, v, kb, vb = qkv_proj(x, g[1], wqkv)
                new_ks.append(k.reshape(b, t, SB_HEADS, SB_HEAD_DIM))
                new_vs.append(v.reshape(b, t, SB_HEADS, SB_HEAD_DIM))
                q3, kb3, vb3 = (arr.reshape(b, t, d) for arr in (q, kb, vb))
                a = sb_attention_sample(q3, kb3, vb3, cache_kt, cache_vt, j).reshape(b * t, d)
            else:
                if grp == 'p':
                    h0 = jnp.zeros((b, 1, D_RNN), F32)
                    c0 = jnp.zeros((b, CONV_W - 1, D_RNN), F32)
                else:
                    h0 = state_lru_h[j].reshape(b, 1, D_RNN)
                    c0 = state_lru_conv[j]
                a, hn, cn = lru_core(x.reshape(b, t, d), g[1], w_in, lru_conv_w[j], vecs[0], w_gates, vecs[1],
                                     vecs[2], vecs[3], h0, c0)
                new_h[grp].append(hn.reshape(b, D_RNN))
                new_c[grp].append(cn)
                a = a.reshape(b * t, D_RNN)
            xs[grp] = ffn2_ple(x, a, wo, ps[grp], i, g[2], wg2, wu2, wd2, g[3], wpe, wpg, gf,
                               final_norm=(i == depth - 1))

    def time_major(kvt):
        return jnp.transpose(kvt.reshape(n_sb, bp, SB_HEADS, SB_HEAD_DIM, tp), (0, 1, 4, 2, 3))

    return (xs['p'].reshape(bp, tp, d), xs['s'].reshape(bs, ts, d),
            time_major(kv_prompt[0]), time_major(kv_prompt[1]), jnp.stack(new_h['p']), jnp.stack(new_c['p']),
            jnp.stack(new_ks), jnp.stack(new_vs), jnp.stack(new_h['s']), jnp.stack(new_c['s']))
```

```python
import functools
import math

import jax
import jax.numpy as jnp
from jax import lax
from jax.experimental import pallas as pl
from jax.experimental.pallas import tpu as pltpu

D_MODEL = 1024
SB_HEADS = 16
SB_HEAD_DIM = 64
D_RNN = 1280
LRU_BLOCKS = 16
LRU_BLOCK = 80
CONV_W = 4
LRU_C = 8.0
D_FF = 2816
PLE_DIM = 256
EPS = 1e-6
LOG2E = 1.4426950408889634

BF16 = jnp.bfloat16
F32 = jnp.float32

V7X_VMEM_LIMIT = 56 * 1024 * 1024
V7X_MXU_COLS = 256


def _const_spec(shape):
    nd = len(shape)
    return pl.BlockSpec(shape, lambda *_: (0,) * nd, pipeline_mode=pl.Buffered(1))


def _rms(x, g):
    ms = jnp.mean(x * x, axis=-1, keepdims=True)
    return x * lax.rsqrt(ms + EPS) * g


def _dot(a, b):
    return jnp.dot(a, b, preferred_element_type=F32)


def _nt_dot(a, b):
    return lax.dot_general(a, b, (((1,), (1,)), ((), ())), preferred_element_type=F32)


def _tn_dot(a, b):
    return lax.dot_general(a, b, (((0,), (0,)), ((), ())), preferred_element_type=F32)


def _sigmoid(x):
    return 0.5 * jnp.tanh(0.5 * x) + 0.5


def _cparams(sem):
    return pltpu.CompilerParams(dimension_semantics=sem, vmem_limit_bytes=V7X_VMEM_LIMIT)


FF_CHUNK = 256


def _swiglu_half(x, g, wg_ref, wu_ref, wd_ref):
    xn = _rms(x, g).astype(BF16)
    acc = jnp.zeros(x.shape, F32)
    for c in range(D_FF // FF_CHUNK):
        sl = slice(c * FF_CHUNK, (c + 1) * FF_CHUNK)
        gate = _dot(xn, wg_ref[:, sl])
        up = _dot(xn, wu_ref[:, sl])
        h = (gate * _sigmoid(gate) * up).astype(BF16)
        acc = acc + _dot(h, wd_ref[sl, :])
    return x + 0.5 * acc


def _ffn1_kernel(x_ref, g_ref, wg_ref, wu_ref, wd_ref, o_ref):
    o_ref[...] = _swiglu_half(x_ref[...], g_ref[...], wg_ref, wu_ref, wd_ref)


def _ffn2_ple_kernel(x_ref, a_ref, wo_ref, p_ref, g2_ref, wg_ref, wu_ref, wd_ref, g3_ref, wpe_ref, wpg_ref,
                     gf_ref, o_ref, *, final_norm):
    x = x_ref[...] + _dot(a_ref[...], wo_ref[...])
    pe = _dot(p_ref[...].astype(BF16), wpe_ref[...])
    x = _swiglu_half(x, g2_ref[...], wg_ref, wu_ref, wd_ref)
    xn = _rms(x, g3_ref[...]).astype(BF16)
    gate = _sigmoid(_dot(xn, wpg_ref[...]))
    x = x + pe * gate
    if final_norm:
        x = _rms(x, gf_ref[...])
    o_ref[...] = x


def _token_tile(n):
    if n % 512 == 0 and n // 512 >= 4:
        return 512
    assert n % 256 == 0
    return n // 2


def _tok_spec(tm, width):
    return pl.BlockSpec((tm, width), lambda i: (i, 0))


def ffn1(x, g, wg, wu, wd):
    n = x.shape[0]
    tm = _token_tile(n)
    return pl.pallas_call(
        _ffn1_kernel,
        grid=(n // tm,),
        in_specs=[_tok_spec(tm, D_MODEL), _const_spec((1, D_MODEL)), _const_spec(wg.shape), _const_spec(wu.shape),
                  _const_spec(wd.shape)],
        out_specs=_tok_spec(tm, D_MODEL),
        out_shape=jax.ShapeDtypeStruct(x.shape, F32),
        compiler_params=_cparams(("parallel",)),
        name="ffn1",
    )(x, g, wg, wu, wd)


def ffn2_ple(x, a, wo, p_all, layer, g2, wg, wu, wd, g3, wpe, wpg, gf, final_norm):
    n = x.shape[0]
    tm = _token_tile(n)
    ka = a.shape[1]
    p_spec = pl.BlockSpec((None, tm, PLE_DIM), lambda i: (layer, i, 0))
    return pl.pallas_call(
        functools.partial(_ffn2_ple_kernel, final_norm=final_norm),
        grid=(n // tm,),
        in_specs=[_tok_spec(tm, D_MODEL), _tok_spec(tm, ka), _const_spec(wo.shape), p_spec,
                  _const_spec((1, D_MODEL)), _const_spec(wg.shape), _const_spec(wu.shape), _const_spec(wd.shape),
                  _const_spec((1, D_MODEL)), _const_spec(wpe.shape), _const_spec(wpg.shape),
                  _const_spec((1, D_MODEL))],
        out_specs=_tok_spec(tm, D_MODEL),
        out_shape=jax.ShapeDtypeStruct(x.shape, F32),
        compiler_params=_cparams(("parallel",)),
        name="ffn2_ple",
    )(x, a, wo, p_all, g2, wg, wu, wd, g3, wpe, wpg, gf)


Q_SCALE = -(SB_HEAD_DIM ** -0.5) * LOG2E


def _qkv_kernel(x_ref, g_ref, w_ref, q_ref, k_ref, v_ref, kb_ref, vb_ref):
    xn = _rms(x_ref[...], g_ref[...]).astype(BF16)
    d = D_MODEL
    q_ref[...] = (_dot(xn, w_ref[:, 0:d]) * Q_SCALE).astype(BF16)
    k = _dot(xn, w_ref[:, d:2 * d])
    k_ref[...] = k
    kb_ref[...] = k.astype(BF16)
    v = _dot(xn, w_ref[:, 2 * d:3 * d])
    v_ref[...] = v
    vb_ref[...] = v.astype(BF16)


def qkv_proj(x, g, w):
    n = x.shape[0]
    tm = _token_tile(n)
    tok = _tok_spec(tm, D_MODEL)
    return pl.pallas_call(
        _qkv_kernel,
        grid=(n // tm,),
        in_specs=[tok, _const_spec((1, D_MODEL)), _const_spec(w.shape)],
        out_specs=[tok] * 5,
        out_shape=[jax.ShapeDtypeStruct(x.shape, BF16), jax.ShapeDtypeStruct(x.shape, F32),
                   jax.ShapeDtypeStruct(x.shape, F32), jax.ShapeDtypeStruct(x.shape, BF16),
                   jax.ShapeDtypeStruct(x.shape, BF16)],
        compiler_params=_cparams(("parallel",)),
        name="qkv_proj",
    )(x, g, w)


def _qkv_t_kernel(x_ref, g_ref, wq_ref, wkt_ref, wvt_ref, *rest):
    q_ref, kt_ref, vt_ref, kb_ref, vtb_ref = rest[-5:]
    xn = _rms(x_ref[0], g_ref[...]).astype(BF16)
    q_ref[0] = (_dot(xn, wq_ref[...]) * Q_SCALE).astype(BF16)
    kt = _nt_dot(wkt_ref[...], xn)
    kt_ref[0, 0] = kt
    kb_ref[0] = kt.T.astype(BF16)
    vt = _nt_dot(wvt_ref[...], xn)
    vt_ref[0, 0] = vt
    vtb_ref[0] = vt.astype(BF16)


def qkv_proj_t(x, g, wq, wkt, wvt, slot, n_slots, prev=None):
    b, t, d = x.shape
    tm = _token_tile(t)
    tok = pl.BlockSpec((1, tm, d), lambda i, j: (i, j, 0))
    feat = pl.BlockSpec((1, d, tm), lambda i, j: (i, 0, j))
    slot_spec = pl.BlockSpec((1, 1, d, tm), lambda i, j: (slot, i, 0, j))
    in_specs = [tok, _const_spec((1, d)), _const_spec(wq.shape), _const_spec(wkt.shape), _const_spec(wvt.shape)]
    args = [x, g, wq, wkt, wvt]
    aliases = {}
    if prev is not None:
        in_specs += [pl.BlockSpec(memory_space=pl.ANY)] * 2
        aliases = {len(args): 1, len(args) + 1: 2}
        args += list(prev)
    slots = jax.ShapeDtypeStruct((n_slots, b, d, t), F32)
    return pl.pallas_call(
        _qkv_t_kernel,
        grid=(b, t // tm),
        in_specs=in_specs,
        out_specs=[tok, slot_spec, slot_spec, tok, feat],
        out_shape=[jax.ShapeDtypeStruct((b, t, d), BF16), slots, slots,
                   jax.ShapeDtypeStruct((b, t, d), BF16), jax.ShapeDtypeStruct((b, d, t), BF16)],
        input_output_aliases=aliases,
        compiler_params=_cparams(("parallel", "parallel")),
        name="qkv_proj_t",
    )(*args)


ATT_TK = 256
ATT_GROUP = 4


def _upper_ones(n):
    r = lax.broadcasted_iota(jnp.int32, (n, n), 0)
    c = lax.broadcasted_iota(jnp.int32, (n, n), 1)
    return jnp.where(c > r, 1.0, 0.0).astype(BF16)


def _log2_stay(nz):
    neg_abs = lax.bitcast_convert_type(lax.bitcast_convert_type(nz, jnp.int32) | jnp.int32(-2 ** 31), F32)
    return jnp.minimum(nz, 0.0) - jnp.log2(1.0 + jnp.exp2(neg_abs))


STICK_FLOOR = -160.0


def _sb_weights(nz, visibles, uppers, carries):
    n_t, n_g = len(nz), len(nz[0])
    stay = [[None] * n_g for _ in range(n_t)]
    after = [[None] * n_g for _ in range(n_t)]
    for t in range(n_t):
        for g in range(n_g):
            if nz[t][g] is None:
                continue
            s = _log2_stay(nz[t][g])
            if visibles[t][g] is not None:
                s = jnp.where(visibles[t][g], s, 0.0)
            stay[t][g] = s
            after[t][g] = _dot(uppers[t], s.astype(BF16))
    weights = [[None] * n_g for _ in range(n_t)]
    new_carries = []
    for g in range(n_g):
        carry = carries[g]
        for t in range(n_t):
            if nz[t][g] is None:
                continue
            w = jnp.exp2((stay[t][g] - nz[t][g]) + after[t][g] + carry)
            if visibles[t][g] is not None:
                w = jnp.where(visibles[t][g], w, 0.0)
            weights[t][g] = w.astype(BF16)
            carry = carry + after[t][g][0:1, :] + stay[t][g][0:1, :]
        new_carries.append(carry)
    return weights, new_carries


def _sb_prompt_kernel(q_ref, k_ref, vt_ref, o_ref, acc_ref, carry_ref, live_ref):
    n_tiles = q_ref.shape[1] // ATT_TK
    upper = _upper_ones(ATT_TK)
    lane = lax.broadcasted_iota(jnp.int32, (1, 128), 1)
    head_a = lane < SB_HEAD_DIM
    kpos = lax.broadcasted_iota(jnp.int32, (ATT_TK, ATT_TK), 0)
    qpos = lax.broadcasted_iota(jnp.int32, (ATT_TK, ATT_TK), 1)
    causal = kpos < qpos
    row = lax.broadcasted_iota(jnp.int32, (128, 1), 0)

    def k_tile(i):
        return k_ref[0, pl.ds(pl.multiple_of(i * ATT_TK, ATT_TK), ATT_TK), :]

    def v_tile(i):
        return vt_ref[0, :, pl.ds(pl.multiple_of(i * ATT_TK, ATT_TK), ATT_TK)]

    def head_queries(i, n):
        q2 = q_ref[0, pl.ds(pl.multiple_of(i * ATT_TK, ATT_TK), n * ATT_TK), :]
        return [jnp.where(head_a, q2, jnp.zeros_like(q2)), jnp.where(head_a, jnp.zeros_like(q2), q2)]

    def stick_left(carries):
        return jnp.max(jnp.maximum(carries[0], carries[1])) > STICK_FLOOR

    def write_out(i):
        out_t = jnp.where(row < SB_HEAD_DIM, acc_ref[i, 0], acc_ref[i, 1])
        o_ref[0, pl.ds(pl.multiple_of(i * ATT_TK, ATT_TK), ATT_TK), :] = out_t.T.astype(BF16)

    def group(first, with_prev):
        m = ATT_GROUP
        heads = head_queries(first, m)
        own = [[None, None] for _ in range(m)]
        prev = [[None, None] for _ in range(m)]
        for i in range(-1 if with_prev else 0, m):
            lo, hi = max(i, 0), min(i + 1, m - 1)
            for h in range(2):
                nz = _nt_dot(k_tile(first + i), heads[h][lo * ATT_TK:(hi + 1) * ATT_TK])
                if i >= 0:
                    own[i][h] = nz[:, 0:ATT_TK]
                if i + 1 < m:
                    prev[i + 1][h] = nz[:, (i + 1 - lo) * ATT_TK:(i + 2 - lo) * ATT_TK]
        flat = lambda per_tile: [per_tile[j][h] for j in range(m) for h in range(2)]
        n_g = 2 * m
        zero = jnp.zeros((1, ATT_TK), F32)
        w, carries = _sb_weights([flat(own), flat(prev)], [[causal] * n_g, [None] * n_g], [upper, upper],
                                 [zero] * n_g)
        for j in range(m):
            for h in range(2):
                g = 2 * j + h
                out = _dot(v_tile(first + j), w[0][g])
                if w[1][g] is not None:
                    out = out + _dot(v_tile(first + j - 1), w[1][g])
                acc_ref[first + j, h] = out
                carry_ref[first + j, h] = jnp.broadcast_to(carries[g], (8, ATT_TK))
            live_ref[first + j] = stick_left(carries[2 * j:2 * j + 2]).astype(jnp.int32)
            write_out(first + j)

    group(0, False)

    def later_group(gi, _):
        group(gi * ATT_GROUP, True)
        return 0

    lax.fori_loop(1, n_tiles // ATT_GROUP, later_group, 0)

    def earlier_keys(i, _):
        def more_keys(state):
            n, live = state
            return jnp.logical_and(n < i - 1, live)

        def k_block(state):
            n, _ = state
            kt = i - 2 - n
            heads = head_queries(i, 1)
            nz = [[_nt_dot(k_tile(kt), qh) for qh in heads]]
            w, carries = _sb_weights(nz, [[None, None]], [upper], [carry_ref[i, 0, 0:1, :], carry_ref[i, 1, 0:1, :]])
            for h in range(2):
                acc_ref[i, h] += _dot(v_tile(kt), w[0][h])
                carry_ref[i, h] = jnp.broadcast_to(carries[h], (8, ATT_TK))
            return n + 1, stick_left(carries)

        n_done, _ = lax.while_loop(more_keys, k_block, (jnp.int32(0), live_ref[i] != 0))

        @pl.when(n_done > 0)
        def _():
            write_out(i)

        return 0

    lax.fori_loop(2, n_tiles, earlier_keys, 0)


def sb_attention_prompt(q, k, vt):
    b, t, d = q.shape
    assert t % (ATT_GROUP * ATT_TK) == 0
    n_tiles = t // ATT_TK
    tok = pl.BlockSpec((1, t, 128), lambda i, j: (i, 0, j))
    feat = pl.BlockSpec((1, 128, t), lambda i, j: (i, j, 0))
    return pl.pallas_call(
        _sb_prompt_kernel,
        grid=(b, d // 128),
        in_specs=[tok, tok, feat],
        out_specs=tok,
        out_shape=jax.ShapeDtypeStruct(q.shape, BF16),
        scratch_shapes=[pltpu.VMEM((n_tiles, 2, 128, ATT_TK), F32), pltpu.VMEM((n_tiles, 2, 8, ATT_TK), F32),
                        pltpu.SMEM((n_tiles,), jnp.int32)],
        compiler_params=_cparams(("parallel", "parallel")),
        name="sb_attn_prompt",
    )(q, k, vt)


SAMPLE_HEADS = 4
SAMPLE_W = SAMPLE_HEADS * SB_HEAD_DIM


SAMPLE_GROUPS = 2


def _sample_queries(q_ref):
    tq = q_ref.shape[1]
    lanes = SAMPLE_HEADS * tq
    r_head = lax.broadcasted_iota(jnp.int32, (lanes, SAMPLE_W), 0) // tq
    c_head = lax.broadcasted_iota(jnp.int32, (lanes, SAMPLE_W), 1) // SB_HEAD_DIM
    own = r_head == c_head
    qm = []
    for g in range(SAMPLE_GROUPS):
        qrows = jnp.concatenate([q_ref[0, :, g * SAMPLE_W:(g + 1) * SAMPLE_W]] * SAMPLE_HEADS, axis=0)
        qm.append(jnp.where(own, qrows, jnp.zeros_like(qrows)))
    return own, qm


def _sample_write_out(o_ref, acc, own, g):
    tq = o_ref.shape[1]
    acc_t = jnp.where(own, acc.T, 0.0)
    out = acc_t[0:tq]
    for h in range(1, SAMPLE_HEADS):
        out = out + acc_t[h * tq:(h + 1) * tq]
    o_ref[0, :, g * SAMPLE_W:(g + 1) * SAMPLE_W] = out.astype(BF16)


def _sb_sample_first_kernel(q_ref, kn_ref, vn_ref, ckt_ref, cvt_ref, o_ref, acc_ref, carry_ref):
    tq = q_ref.shape[1]
    lanes = SAMPLE_HEADS * tq
    n_grp = SAMPLE_GROUPS
    cols = [slice(g * SAMPLE_W, (g + 1) * SAMPLE_W) for g in range(n_grp)]
    upper = _upper_ones(ATT_TK)
    own, qm = _sample_queries(q_ref)
    kpos = lax.broadcasted_iota(jnp.int32, (tq, lanes), 0)
    qpos = lax.broadcasted_iota(jnp.int32, (tq, lanes), 1) % tq
    visible = kpos < qpos
    nz = [[_nt_dot(kn_ref[0, :, cols[g]], qm[g]) for g in range(n_grp)],
          [_nt_dot(ckt_ref[0, 0, cols[g], :].T.astype(BF16), qm[g]) for g in range(n_grp)]]
    w, carries = _sb_weights(nz, [[visible] * n_grp, [None] * n_grp], [upper[0:tq, 0:tq], upper],
                             [jnp.zeros((1, lanes), F32)] * n_grp)
    for g in range(n_grp):
        acc = _tn_dot(vn_ref[0, :, cols[g]], w[0][g]) + _dot(cvt_ref[0, 0, cols[g], :].astype(BF16), w[1][g])
        acc_ref[0, g] = acc
        carry_ref[0, g] = jnp.broadcast_to(carries[g], (8, lanes))
        _sample_write_out(o_ref, acc, own, g)


def _sb_sample_rest_kernel(q_ref, ckt_ref, cvt_ref, acc_in_ref, carry_in_ref, o_ref, acc_ref, carry_ref):
    past = ckt_ref.shape[3]
    lanes = SAMPLE_HEADS * q_ref.shape[1]
    n_grp = SAMPLE_GROUPS
    cols = [slice(g * SAMPLE_W, (g + 1) * SAMPLE_W) for g in range(n_grp)]
    upper = _upper_ones(ATT_TK)
    own, qm = _sample_queries(q_ref)
    acc_ref[...] = acc_in_ref[0]
    carry_ref[...] = carry_in_ref[0]

    def stick_left(carries):
        m = carries[0]
        for c in carries[1:]:
            m = jnp.maximum(m, c)
        return jnp.max(m) > STICK_FLOOR

    def more_keys(state):
        n, live = state
        return jnp.logical_and(n < past // ATT_TK - 1, live)

    def k_block(state):
        n, _ = state
        start = pl.multiple_of(past - (n + 2) * ATT_TK, ATT_TK)
        nz = [[_nt_dot(ckt_ref[0, 0, cols[g], pl.ds(start, ATT_TK)].T.astype(BF16), qm[g]) for g in range(n_grp)]]
        w, carries = _sb_weights(nz, [[None] * n_grp], [upper], [carry_ref[g, 0:1, :] for g in range(n_grp)])
        for g in range(n_grp):
            acc_ref[g] += _dot(cvt_ref[0, 0, cols[g], pl.ds(start, ATT_TK)].astype(BF16), w[0][g])
            carry_ref[g] = jnp.broadcast_to(carries[g], (8, lanes))
        return n + 1, stick_left(carries)

    lax.while_loop(more_keys, k_block,
                   (jnp.int32(0), stick_left([carry_in_ref[0, g, 0:1, :] for g in range(n_grp)])))
    for g in range(n_grp):
        _sample_write_out(o_ref, acc_ref[g], own, g)


def sb_attention_sample(q, kn, vn, cache_kt, cache_vt, slot):
    b, tq, d = q.shape
    past = cache_kt.shape[3]
    width = SAMPLE_GROUPS * SAMPLE_W
    assert past % ATT_TK == 0 and d % width == 0
    lanes = SAMPLE_HEADS * tq
    n_steps = d // width
    new_spec = pl.BlockSpec((1, tq, width), lambda i, j: (i, 0, j))
    last_spec = pl.BlockSpec((1, 1, width, ATT_TK), lambda i, j: (slot, i, j, past // ATT_TK - 1))
    cache_spec = pl.BlockSpec((1, 1, width, past), lambda i, j: (slot, i, j, 0))
    acc_spec = pl.BlockSpec((1, SAMPLE_GROUPS, SAMPLE_W, lanes), lambda i, j: (i, j, 0, 0))
    carry_spec = pl.BlockSpec((1, SAMPLE_GROUPS, 8, lanes), lambda i, j: (i, j, 0, 0))
    acc_shape = jax.ShapeDtypeStruct((b, n_steps * SAMPLE_GROUPS, SAMPLE_W, lanes), F32)
    carry_shape = jax.ShapeDtypeStruct((b, n_steps * SAMPLE_GROUPS, 8, lanes), F32)
    out, acc, carry = pl.pallas_call(
        _sb_sample_first_kernel,
        grid=(b, n_steps),
        in_specs=[new_spec, new_spec, new_spec, last_spec, last_spec],
        out_specs=[new_spec, acc_spec, carry_spec],
        out_shape=[jax.ShapeDtypeStruct(q.shape, BF16), acc_shape, carry_shape],
        compiler_params=_cparams(("parallel", "parallel")),
        name="sb_attn_sample_first",
    )(q, kn, vn, cache_kt, cache_vt)
    if past == ATT_TK:
        return out

    def rest():
        return pl.pallas_call(
            _sb_sample_rest_kernel,
            grid=(b, n_steps),
            in_specs=[new_spec, cache_spec, cache_spec, acc_spec, carry_spec],
            out_specs=new_spec,
            out_shape=jax.ShapeDtypeStruct(q.shape, BF16),
            scratch_shapes=[pltpu.VMEM((SAMPLE_GROUPS, SAMPLE_W, lanes), F32),
                            pltpu.VMEM((SAMPLE_GROUPS, 8, lanes), F32)],
            compiler_params=_cparams(("parallel", "parallel")),
            name="sb_attn_sample_rest",
        )(q, cache_kt, cache_vt, acc, carry)

    return lax.cond(jnp.max(carry) > STICK_FLOOR, rest, lambda: out)


TAIL_ROWS = 8
GATE_K = 512
LRU_SUB = 256
LRU_SUBS_PER_STEP = 2


def _gate_tile_starts():
    starts = []
    for n in range(D_RNN // V7X_MXU_COLS):
        first = (n * V7X_MXU_COLS) // LRU_BLOCK * LRU_BLOCK
        last = ((n + 1) * V7X_MXU_COLS - 1) // LRU_BLOCK * LRU_BLOCK + LRU_BLOCK
        lo = min(first // 128 * 128, D_RNN - GATE_K)
        assert lo <= first and last <= lo + GATE_K
        starts.append(lo)
    return starts


def _gate_slabs(w_a, w_x):
    def dense(w):
        eye = jnp.eye(LRU_BLOCKS, dtype=w.dtype)
        return jnp.einsum('njk,nm->njmk', w, eye).reshape(D_RNN, D_RNN)
    da, dx = dense(w_a), dense(w_x)
    slabs = []
    for n, lo in enumerate(_gate_tile_starts()):
        cols = slice(n * V7X_MXU_COLS, (n + 1) * V7X_MXU_COLS)
        slabs.append(jnp.concatenate([da[lo:lo + GATE_K, cols], dx[lo:lo + GATE_K, cols]], axis=1))
    return jnp.stack(slabs).astype(BF16)


def _gelu_tanh(x):
    c = math.sqrt(2.0 / math.pi)
    half_x = 0.5 * x
    return half_x + half_x * jnp.tanh(x * (c + (0.044715 * c) * (x * x)))


def _lru_kernel(x_ref, g_ref, win_ref, cw_ref, cb_ref, wg_ref, ba_ref, bx_ref, lam_ref, h0_ref, c0_ref,
                y_ref, hn_ref, cn_ref, ext_ref, a_ref, b_ref, h_ref, hstate_ref):
    tc = x_ref.shape[1]
    n_sub, sub = a_ref.shape[0], a_ref.shape[1]
    step = pl.program_id(1)

    @pl.when(step == 0)
    def _():
        hstate_ref[...] = h0_ref[0]
        ext_ref[0:TAIL_ROWS - (CONV_W - 1), :] = jnp.zeros((TAIL_ROWS - (CONV_W - 1), D_RNN), F32)
        ext_ref[TAIL_ROWS - (CONV_W - 1):TAIL_ROWS, :] = c0_ref[0]

    lam = lam_ref[...]
    softplus_neg_lam = jnp.maximum(-lam, 0.0) + jnp.log(1.0 + jnp.exp(-jnp.abs(lam)))
    decay = -LRU_C * softplus_neg_lam
    h = hstate_ref[...]
    for s in range(n_sub):
        r0 = s * sub
        xn = _rms(x_ref[0, r0:r0 + sub, :], g_ref[...]).astype(BF16)
        gx = _dot(xn, win_ref[...])
        gate = _gelu_tanh(gx[:, :D_RNN])
        u = gx[:, D_RNN:]
        ext_ref[TAIL_ROWS + r0:TAIL_ROWS + r0 + sub, :] = u
        c = cb_ref[...] + u * cw_ref[CONV_W - 1:CONV_W, :]
        for j in range(CONV_W - 1):
            first = TAIL_ROWS + r0 - (CONV_W - 1 - j)
            c = c + ext_ref[first:first + sub, :] * cw_ref[j:j + 1, :]
        cbf = c.astype(BF16)
        pre = [_dot(cbf[:, lo:lo + GATE_K], wg_ref[n]) for n, lo in enumerate(_gate_tile_starts())]
        r = _sigmoid(jnp.concatenate([p[:, :V7X_MXU_COLS] for p in pre], axis=-1) + ba_ref[...])
        gin = _sigmoid(jnp.concatenate([p[:, V7X_MXU_COLS:] for p in pre], axis=-1) + bx_ref[...])
        log_a = decay * r
        a = jnp.exp(log_a)
        a_ref[s] = a
        one_minus_a2 = -jnp.tanh(log_a) * (a * a + 1.0)
        root = jnp.where(one_minus_a2 > 0.0, one_minus_a2 * lax.rsqrt(one_minus_a2), 0.0)
        b_ref[s] = root * (gin * c)
        for i in range(sub):
            h = a_ref[s, i:i + 1, :] * h + b_ref[s, i:i + 1, :]
            h_ref[s, i:i + 1, :] = h
        y_ref[0, r0:r0 + sub, :] = (h_ref[s] * gate).astype(BF16)

    new_tail = ext_ref[tc + TAIL_ROWS - (CONV_W - 1):tc + TAIL_ROWS, :]
    ext_ref[TAIL_ROWS - (CONV_W - 1):TAIL_ROWS, :] = new_tail
    cn_ref[0] = new_tail
    hstate_ref[...] = h
    hn_ref[0] = h


def lru_core(x, g, w_in, conv_w, conv_b, w_gates, b_a, b_x, lam, h0, conv0):
    b, t, d = x.shape
    sub = min(LRU_SUB, t)
    tc = min(LRU_SUBS_PER_STEP * sub, t)
    assert t % tc == 0 and tc % sub == 0
    vec = _const_spec((1, D_RNN))
    return pl.pallas_call(
        _lru_kernel,
        grid=(b, t // tc),
        in_specs=[pl.BlockSpec((1, tc, d), lambda i, j: (i, j, 0)), _const_spec((1, D_MODEL)),
                  _const_spec(w_in.shape), _const_spec((CONV_W, D_RNN)), vec, _const_spec(w_gates.shape), vec,
                  vec, vec,
                  pl.BlockSpec((1, 1, D_RNN), lambda i, j: (i, 0, 0)),
                  pl.BlockSpec((1, CONV_W - 1, D_RNN), lambda i, j: (i, 0, 0))],
        out_specs=[pl.BlockSpec((1, tc, D_RNN), lambda i, j: (i, j, 0)),
                   pl.BlockSpec((1, 1, D_RNN), lambda i, j: (i, 0, 0)),
                   pl.BlockSpec((1, CONV_W - 1, D_RNN), lambda i, j: (i, 0, 0))],
        out_shape=[jax.ShapeDtypeStruct((b, t, D_RNN), BF16), jax.ShapeDtypeStruct((b, 1, D_RNN), F32),
                   jax.ShapeDtypeStruct((b, CONV_W - 1, D_RNN), F32)],
        scratch_shapes=[pltpu.VMEM((tc + TAIL_ROWS, D_RNN), F32)]
        + [pltpu.VMEM((tc // sub, sub, D_RNN), F32)] * 3 + [pltpu.VMEM((1, D_RNN), F32)],
        compiler_params=_cparams(("parallel", "arbitrary")),
        name="lru_core",
    )(x, g, w_in, conv_w, conv_b, w_gates, b_a, b_x, lam, h0, conv0)


def kernel(x_prompt, x_sample, p_prompt, p_sample, cache_sb_k, cache_sb_v, state_lru_h, state_lru_conv, norm_g,
           ffn_w_gate, ffn_w_up, ffn_w_down, ple_w_proj, ple_w_gate, sb_w_qkv, sb_w_o, lru_w_in, lru_conv_w,
           lru_conv_b, lru_w_a, lru_b_a, lru_w_x, lru_b_x, lru_lambda, lru_w_o, final_norm_g):
    depth = norm_g.shape[0]
    bp, tp, d = x_prompt.shape
    bs, ts, _ = x_sample.shape
    n_sb, _, past = cache_sb_k.shape[:3]
    xs = {'p': x_prompt.reshape(bp * tp, d), 's': x_sample.reshape(bs * ts, d)}
    ps = {'p': p_prompt.reshape(depth, bp * tp, PLE_DIM), 's': p_sample.reshape(depth, bs * ts, PLE_DIM)}
    dims = {'p': (bp, tp), 's': (bs, ts)}
    gf = final_norm_g.reshape(1, d)
    cache_kt = jnp.transpose(cache_sb_k, (0, 1, 3, 4, 2)).reshape(n_sb, bs, d, past)
    cache_vt = jnp.transpose(cache_sb_v, (0, 1, 3, 4, 2)).reshape(n_sb, bs, d, past)
    kv_prompt = None
    new_ks, new_vs = [], []
    new_h = {'p': [], 's': []}
    new_c = {'p': [], 's': []}

    for i in range(depth):
        j = i // 2
        g = norm_g[i].reshape(4, 1, d)
        wg1, wu1, wd1 = (w[i, 0].astype(BF16) for w in (ffn_w_gate, ffn_w_up, ffn_w_down))
        wg2, wu2, wd2 = (w[i, 1].astype(BF16) for w in (ffn_w_gate, ffn_w_up, ffn_w_down))
        wpe = ple_w_proj[i].astype(BF16)
        wpg = ple_w_gate[i].astype(BF16)
        if i % 2 == 0:
            wqkv = sb_w_qkv[j].astype(BF16)
            wq, wkt, wvt = wqkv[:, :d], wqkv[:, d:2 * d].T, wqkv[:, 2 * d:].T
            wo = sb_w_o[j].astype(BF16)
        else:
            w_in = lru_w_in[j].astype(BF16)
            w_gates = _gate_slabs(lru_w_a[j], lru_w_x[j])
            wo = lru_w_o[j].astype(BF16)
            vecs = [v[j].reshape(1, D_RNN) for v in (lru_conv_b, lru_b_a, lru_b_x, lru_lambda)]
        for grp in ('p', 's'):
            b, t = dims[grp]
            x = ffn1(xs[grp], g[0], wg1, wu1, wd1)
            if i % 2 == 0 and grp == 'p':
                q, kt, vt, kb, vtb = qkv_proj_t(x.reshape(b, t, d), g[1], wq, wkt, wvt, j, n_sb, kv_prompt)
                kv_prompt = (kt, vt)
                a = sb_attention_prompt(q, kb, vtb).reshape(b * t, d)
            elif i % 2 == 0:
                q, k, v, kb, vb = qkv_proj(x, g[1], wqkv)
                new_ks.append(k.reshape(b, t, SB_HEADS, SB_HEAD_DIM))
                new_vs.append(v.reshape(b, t, SB_HEADS, SB_HEAD_DIM))
                q3, kb3, vb3 = (arr.reshape(b, t, d) for arr in (q, kb, vb))
                a = sb_attention_sample(q3, kb3, vb3, cache_kt, cache_vt, j).reshape(b * t, d)
            else:
                if grp == 'p':
                    h0 = jnp.zeros((b, 1, D_RNN), F32)
                    c0 = jnp.zeros((b, CONV_W - 1, D_RNN), F32)
                else:
                    h0 = state_lru_h[j].reshape(b, 1, D_RNN)
                    c0 = state_lru_conv[j]
                a, hn, cn = lru_core(x.reshape(b, t, d), g[1], w_in, lru_conv_w[j], vecs[0], w_gates, vecs[1],
                                     vecs[2], vecs[3], h0, c0)
                new_h[grp].append(hn.reshape(b, D_RNN))
                new_c[grp].append(cn)
                a = a.reshape(b * t, D_RNN)
            xs[grp] = ffn2_ple(x, a, wo, ps[grp], i, g[2], wg2, wu2, wd2, g[3], wpe, wpg, gf,
                               final_norm=(i == depth - 1))

    def time_major(kvt):
        return jnp.transpose(kvt.reshape(n_sb, bp, SB_HEADS, SB_HEAD_DIM, tp), (0, 1, 4, 2, 3))

    return (xs['p'].reshape(bp, tp, d), xs['s'].reshape(bs, ts, d),
            time_major(kv_prompt[0]), time_major(kv_prompt[1]), jnp.stack(new_h['p']), jnp.stack(new_c['p']),
            jnp.stack(new_ks), jnp.stack(new_vs), jnp.stack(new_h['s']), jnp.stack(new_c['s']))
```

```python
import functools
import math

import jax
import jax.numpy as jnp
from jax import lax
from jax.experimental import pallas as pl
from jax.experimental.pallas import tpu as pltpu

D_MODEL = 1024
SB_HEADS = 16
SB_HEAD_DIM = 64
D_RNN = 1280
LRU_BLOCKS = 16
LRU_BLOCK = 80
CONV_W = 4
LRU_C = 8.0
D_FF = 2816
PLE_DIM = 256
EPS = 1e-6
LOG2E = 1.4426950408889634

BF16 = jnp.bfloat16
F32 = jnp.float32

V7X_VMEM_LIMIT = 56 * 1024 * 1024
V7X_MXU_COLS = 256


def _const_spec(shape):
    nd = len(shape)
    return pl.BlockSpec(shape, lambda *_: (0,) * nd, pipeline_mode=pl.Buffered(1))


def _rms(x, g):
    ms = jnp.mean(x * x, axis=-1, keepdims=True)
    return x * lax.rsqrt(ms + EPS) * g


def _dot(a, b):
    return jnp.dot(a, b, preferred_element_type=F32)


def _nt_dot(a, b):
    return lax.dot_general(a, b, (((1,), (1,)), ((), ())), preferred_element_type=F32)


def _tn_dot(a, b):
    return lax.dot_general(a, b, (((0,), (0,)), ((), ())), preferred_element_type=F32)


def _sigmoid(x):
    return 0.5 * jnp.tanh(0.5 * x) + 0.5


def _cparams(sem):
    return pltpu.CompilerParams(dimension_semantics=sem, vmem_limit_bytes=V7X_VMEM_LIMIT)


FF_CHUNK = 256


def _swiglu_half(x, g, wg_ref, wu_ref, wd_ref):
    xn = _rms(x, g).astype(BF16)
    acc = jnp.zeros(x.shape, F32)
    for c in range(D_FF // FF_CHUNK):
        sl = slice(c * FF_CHUNK, (c + 1) * FF_CHUNK)
        gate = _dot(xn, wg_ref[:, sl])
        up = _dot(xn, wu_ref[:, sl])
        h = (gate * _sigmoid(gate) * up).astype(BF16)
        acc = acc + _dot(h, wd_ref[sl, :])
    return x + 0.5 * acc


def _ffn1_kernel(x_ref, g_ref, wg_ref, wu_ref, wd_ref, o_ref):
    o_ref[...] = _swiglu_half(x_ref[...], g_ref[...], wg_ref, wu_ref, wd_ref)


def _ffn2_ple_kernel(x_ref, a_ref, wo_ref, p_ref, g2_ref, wg_ref, wu_ref, wd_ref, g3_ref, wpe_ref, wpg_ref,
                     gf_ref, o_ref, *, final_norm):
    x = x_ref[...] + _dot(a_ref[...], wo_ref[...])
    pe = _dot(p_ref[...].astype(BF16), wpe_ref[...])
    x = _swiglu_half(x, g2_ref[...], wg_ref, wu_ref, wd_ref)
    xn = _rms(x, g3_ref[...]).astype(BF16)
    gate = _sigmoid(_dot(xn, wpg_ref[...]))
    x = x + pe * gate
    if final_norm:
        x = _rms(x, gf_ref[...])
    o_ref[...] = x


def _token_tile(n):
    if n % 512 == 0 and n // 512 >= 4:
        return 512
    assert n % 256 == 0
    return n // 2


def _tok_spec(tm, width):
    return pl.BlockSpec((tm, width), lambda i: (i, 0))


def _stacked_spec(w, *lead):
    shape = (None,) * len(lead) + tuple(w.shape[len(lead):])
    return pl.BlockSpec(shape, lambda *_: tuple(lead) + (0,) * (w.ndim - len(lead)), pipeline_mode=pl.Buffered(1))


def ffn1(x, g, wg_all, wu_all, wd_all, layer):
    n = x.shape[0]
    tm = _token_tile(n)
    return pl.pallas_call(
        _ffn1_kernel,
        grid=(n // tm,),
        in_specs=[_tok_spec(tm, D_MODEL), _const_spec((1, D_MODEL)), _stacked_spec(wg_all, layer, 0),
                  _stacked_spec(wu_all, layer, 0), _stacked_spec(wd_all, layer, 0)],
        out_specs=_tok_spec(tm, D_MODEL),
        out_shape=jax.ShapeDtypeStruct(x.shape, F32),
        compiler_params=_cparams(("parallel",)),
        name="ffn1",
    )(x, g, wg_all, wu_all, wd_all)


def ffn2_ple(x, a, wo, p_all, layer, g2, wg_all, wu_all, wd_all, g3, wpe_all, wpg_all, gf, final_norm):
    n = x.shape[0]
    tm = _token_tile(n)
    ka = a.shape[1]
    p_spec = pl.BlockSpec((None, tm, PLE_DIM), lambda i: (layer, i, 0))
    return pl.pallas_call(
        functools.partial(_ffn2_ple_kernel, final_norm=final_norm),
        grid=(n // tm,),
        in_specs=[_tok_spec(tm, D_MODEL), _tok_spec(tm, ka), _const_spec(wo.shape), p_spec,
                  _const_spec((1, D_MODEL)), _stacked_spec(wg_all, layer, 1), _stacked_spec(wu_all, layer, 1),
                  _stacked_spec(wd_all, layer, 1), _const_spec((1, D_MODEL)), _stacked_spec(wpe_all, layer),
                  _stacked_spec(wpg_all, layer), _const_spec((1, D_MODEL))],
        out_specs=_tok_spec(tm, D_MODEL),
        out_shape=jax.ShapeDtypeStruct(x.shape, F32),
        compiler_params=_cparams(("parallel",)),
        name="ffn2_ple",
    )(x, a, wo, p_all, g2, wg_all, wu_all, wd_all, g3, wpe_all, wpg_all, gf)


Q_SCALE = -(SB_HEAD_DIM ** -0.5) * LOG2E


def _qkv_kernel(x_ref, g_ref, w_ref, q_ref, k_ref, v_ref, kb_ref, vb_ref):
    xn = _rms(x_ref[...], g_ref[...]).astype(BF16)
    d = D_MODEL
    q_ref[...] = (_dot(xn, w_ref[:, 0:d]) * Q_SCALE).astype(BF16)
    k = _dot(xn, w_ref[:, d:2 * d])
    k_ref[...] = k
    kb_ref[...] = k.astype(BF16)
    v = _dot(xn, w_ref[:, 2 * d:3 * d])
    v_ref[...] = v
    vb_ref[...] = v.astype(BF16)


def qkv_proj(x, g, w):
    n = x.shape[0]
    tm = _token_tile(n)
    tok = _tok_spec(tm, D_MODEL)
    return pl.pallas_call(
        _qkv_kernel,
        grid=(n // tm,),
        in_specs=[tok, _const_spec((1, D_MODEL)), _const_spec(w.shape)],
        out_specs=[tok] * 5,
        out_shape=[jax.ShapeDtypeStruct(x.shape, BF16), jax.ShapeDtypeStruct(x.shape, F32),
                   jax.ShapeDtypeStruct(x.shape, F32), jax.ShapeDtypeStruct(x.shape, BF16),
                   jax.ShapeDtypeStruct(x.shape, BF16)],
        compiler_params=_cparams(("parallel",)),
        name="qkv_proj",
    )(x, g, w)


def _qkv_t_kernel(x_ref, g_ref, wq_ref, wkt_ref, wvt_ref, *rest):
    q_ref, kt_ref, vt_ref, kb_ref, vtb_ref = rest[-5:]
    xn = _rms(x_ref[0], g_ref[...]).astype(BF16)
    q_ref[0] = (_dot(xn, wq_ref[...]) * Q_SCALE).astype(BF16)
    kt = _nt_dot(wkt_ref[...], xn)
    kt_ref[0, 0] = kt
    kb_ref[0] = kt.T.astype(BF16)
    vt = _nt_dot(wvt_ref[...], xn)
    vt_ref[0, 0] = vt
    vtb_ref[0] = vt.astype(BF16)


def qkv_proj_t(x, g, wq, wkt, wvt, slot, n_slots, prev=None):
    b, t, d = x.shape
    tm = _token_tile(t)
    tok = pl.BlockSpec((1, tm, d), lambda i, j: (i, j, 0))
    feat = pl.BlockSpec((1, d, tm), lambda i, j: (i, 0, j))
    slot_spec = pl.BlockSpec((1, 1, d, tm), lambda i, j: (slot, i, 0, j))
    in_specs = [tok, _const_spec((1, d)), _const_spec(wq.shape), _const_spec(wkt.shape), _const_spec(wvt.shape)]
    args = [x, g, wq, wkt, wvt]
    aliases = {}
    if prev is not None:
        in_specs += [pl.BlockSpec(memory_space=pl.ANY)] * 2
        aliases = {len(args): 1, len(args) + 1: 2}
        args += list(prev)
    slots = jax.ShapeDtypeStruct((n_slots, b, d, t), F32)
    return pl.pallas_call(
        _qkv_t_kernel,
        grid=(b, t // tm),
        in_specs=in_specs,
        out_specs=[tok, slot_spec, slot_spec, tok, feat],
        out_shape=[jax.ShapeDtypeStruct((b, t, d), BF16), slots, slots,
                   jax.ShapeDtypeStruct((b, t, d), BF16), jax.ShapeDtypeStruct((b, d, t), BF16)],
        input_output_aliases=aliases,
        compiler_params=_cparams(("parallel", "parallel")),
        name="qkv_proj_t",
    )(*args)


ATT_TK = 256
ATT_GROUP = 4


def _upper_ones(n):
    r = lax.broadcasted_iota(jnp.int32, (n, n), 0)
    c = lax.broadcasted_iota(jnp.int32, (n, n), 1)
    return jnp.where(c > r, 1.0, 0.0).astype(BF16)


def _log2_stay(nz):
    neg_abs = lax.bitcast_convert_type(lax.bitcast_convert_type(nz, jnp.int32) | jnp.int32(-2 ** 31), F32)
    return jnp.minimum(nz, 0.0) - jnp.log2(1.0 + jnp.exp2(neg_abs))


STICK_FLOOR = -160.0


def _sb_weights(nz, visibles, uppers, carries):
    n_t, n_g = len(nz), len(nz[0])
    stay = [[None] * n_g for _ in range(n_t)]
    after = [[None] * n_g for _ in range(n_t)]
    for t in range(n_t):
        for g in range(n_g):
            if nz[t][g] is None:
                continue
            s = _log2_stay(nz[t][g])
            if visibles[t][g] is not None:
                s = jnp.where(visibles[t][g], s, 0.0)
            stay[t][g] = s
            after[t][g] = _dot(uppers[t], s.astype(BF16))
    weights = [[None] * n_g for _ in range(n_t)]
    new_carries = []
    for g in range(n_g):
        carry = carries[g]
        for t in range(n_t):
            if nz[t][g] is None:
                continue
            w = jnp.exp2((stay[t][g] - nz[t][g]) + after[t][g] + carry)
            if visibles[t][g] is not None:
                w = jnp.where(visibles[t][g], w, 0.0)
            weights[t][g] = w.astype(BF16)
            carry = carry + after[t][g][0:1, :] + stay[t][g][0:1, :]
        new_carries.append(carry)
    return weights, new_carries


def _sb_prompt_kernel(q_ref, k_ref, vt_ref, o_ref, acc_ref, carry_ref, live_ref):
    n_tiles = q_ref.shape[1] // ATT_TK
    upper = _upper_ones(ATT_TK)
    lane = lax.broadcasted_iota(jnp.int32, (1, 128), 1)
    head_a = lane < SB_HEAD_DIM
    kpos = lax.broadcasted_iota(jnp.int32, (ATT_TK, ATT_TK), 0)
    qpos = lax.broadcasted_iota(jnp.int32, (ATT_TK, ATT_TK), 1)
    causal = kpos < qpos
    row = lax.broadcasted_iota(jnp.int32, (128, 1), 0)

    def k_tile(i):
        return k_ref[0, pl.ds(pl.multiple_of(i * ATT_TK, ATT_TK), ATT_TK), :]

    def v_tile(i):
        return vt_ref[0, :, pl.ds(pl.multiple_of(i * ATT_TK, ATT_TK), ATT_TK)]

    def head_queries(i, n):
        q2 = q_ref[0, pl.ds(pl.multiple_of(i * ATT_TK, ATT_TK), n * ATT_TK), :]
        return [jnp.where(head_a, q2, jnp.zeros_like(q2)), jnp.where(head_a, jnp.zeros_like(q2), q2)]

    def stick_left(carries):
        return jnp.max(jnp.maximum(carries[0], carries[1])) > STICK_FLOOR

    def write_out(i):
        out_t = jnp.where(row < SB_HEAD_DIM, acc_ref[i, 0], acc_ref[i, 1])
        o_ref[0, pl.ds(pl.multiple_of(i * ATT_TK, ATT_TK), ATT_TK), :] = out_t.T.astype(BF16)

    def group(first, with_prev):
        m = ATT_GROUP
        heads = head_queries(first, m)
        own = [[None, None] for _ in range(m)]
        prev = [[None, None] for _ in range(m)]
        for i in range(-1 if with_prev else 0, m):
            lo, hi = max(i, 0), min(i + 1, m - 1)
            for h in range(2):
                nz = _nt_dot(k_tile(first + i), heads[h][lo * ATT_TK:(hi + 1) * ATT_TK])
                if i >= 0:
                    own[i][h] = nz[:, 0:ATT_TK]
                if i + 1 < m:
                    prev[i + 1][h] = nz[:, (i + 1 - lo) * ATT_TK:(i + 2 - lo) * ATT_TK]
        flat = lambda per_tile: [per_tile[j][h] for j in range(m) for h in range(2)]
        n_g = 2 * m
        zero = jnp.zeros((1, ATT_TK), F32)
        w, carries = _sb_weights([flat(own), flat(prev)], [[causal] * n_g, [None] * n_g], [upper, upper],
                                 [zero] * n_g)
        for j in range(m):
            for h in range(2):
                g = 2 * j + h
                out = _dot(v_tile(first + j), w[0][g])
                if w[1][g] is not None:
                    out = out + _dot(v_tile(first + j - 1), w[1][g])
                acc_ref[first + j, h] = out
                carry_ref[first + j, h] = jnp.broadcast_to(carries[g], (8, ATT_TK))
            live_ref[first + j] = stick_left(carries[2 * j:2 * j + 2]).astype(jnp.int32)
            write_out(first + j)

    group(0, False)

    def later_group(gi, _):
        group(gi * ATT_GROUP, True)
        return 0

    lax.fori_loop(1, n_tiles // ATT_GROUP, later_group, 0)

    def earlier_keys(i, _):
        def more_keys(state):
            n, live = state
            return jnp.logical_and(n < i - 1, live)

        def k_block(state):
            n, _ = state
            kt = i - 2 - n
            heads = head_queries(i, 1)
            nz = [[_nt_dot(k_tile(kt), qh) for qh in heads]]
            w, carries = _sb_weights(nz, [[None, None]], [upper], [carry_ref[i, 0, 0:1, :], carry_ref[i, 1, 0:1, :]])
            for h in range(2):
                acc_ref[i, h] += _dot(v_tile(kt), w[0][h])
                carry_ref[i, h] = jnp.broadcast_to(carries[h], (8, ATT_TK))
            return n + 1, stick_left(carries)

        n_done, _ = lax.while_loop(more_keys, k_block, (jnp.int32(0), live_ref[i] != 0))

        @pl.when(n_done > 0)
        def _():
            write_out(i)

        return 0

    lax.fori_loop(2, n_tiles, earlier_keys, 0)


def sb_attention_prompt(q, k, vt):
    b, t, d = q.shape
    assert t % (ATT_GROUP * ATT_TK) == 0
    n_tiles = t // ATT_TK
    tok = pl.BlockSpec((1, t, 128), lambda i, j: (i, 0, j))
    feat = pl.BlockSpec((1, 128, t), lambda i, j: (i, j, 0))
    return pl.pallas_call(
        _sb_prompt_kernel,
        grid=(b, d // 128),
        in_specs=[tok, tok, feat],
        out_specs=tok,
        out_shape=jax.ShapeDtypeStruct(q.shape, BF16),
        scratch_shapes=[pltpu.VMEM((n_tiles, 2, 128, ATT_TK), F32), pltpu.VMEM((n_tiles, 2, 8, ATT_TK), F32),
                        pltpu.SMEM((n_tiles,), jnp.int32)],
        compiler_params=_cparams(("parallel", "parallel")),
        name="sb_attn_prompt",
    )(q, k, vt)


SAMPLE_HEADS = 4
SAMPLE_W = SAMPLE_HEADS * SB_HEAD_DIM


SAMPLE_GROUPS = 2


def _sample_queries(q_ref):
    tq = q_ref.shape[1]
    lanes = SAMPLE_HEADS * tq
    r_head = lax.broadcasted_iota(jnp.int32, (lanes, SAMPLE_W), 0) // tq
    c_head = lax.broadcasted_iota(jnp.int32, (lanes, SAMPLE_W), 1) // SB_HEAD_DIM
    own = r_head == c_head
    qm = []
    for g in range(SAMPLE_GROUPS):
        qrows = jnp.concatenate([q_ref[0, :, g * SAMPLE_W:(g + 1) * SAMPLE_W]] * SAMPLE_HEADS, axis=0)
        qm.append(jnp.where(own, qrows, jnp.zeros_like(qrows)))
    return own, qm


def _sample_write_out(o_ref, acc, own, g):
    tq = o_ref.shape[1]
    acc_t = jnp.where(own, acc.T, 0.0)
    out = acc_t[0:tq]
    for h in range(1, SAMPLE_HEADS):
        out = out + acc_t[h * tq:(h + 1) * tq]
    o_ref[0, :, g * SAMPLE_W:(g + 1) * SAMPLE_W] = out.astype(BF16)


def _sb_sample_first_kernel(q_ref, kn_ref, vn_ref, ckt_ref, cvt_ref, o_ref, acc_ref, carry_ref):
    tq = q_ref.shape[1]
    lanes = SAMPLE_HEADS * tq
    n_grp = SAMPLE_GROUPS
    cols = [slice(g * SAMPLE_W, (g + 1) * SAMPLE_W) for g in range(n_grp)]
    upper = _upper_ones(ATT_TK)
    own, qm = _sample_queries(q_ref)
    kpos = lax.broadcasted_iota(jnp.int32, (tq, lanes), 0)
    qpos = lax.broadcasted_iota(jnp.int32, (tq, lanes), 1) % tq
    visible = kpos < qpos
    nz = [[_nt_dot(kn_ref[0, :, cols[g]], qm[g]) for g in range(n_grp)],
          [_nt_dot(ckt_ref[0, 0, cols[g], :].T.astype(BF16), qm[g]) for g in range(n_grp)]]
    w, carries = _sb_weights(nz, [[visible] * n_grp, [None] * n_grp], [upper[0:tq, 0:tq], upper],
                             [jnp.zeros((1, lanes), F32)] * n_grp)
    for g in range(n_grp):
        acc = _tn_dot(vn_ref[0, :, cols[g]], w[0][g]) + _dot(cvt_ref[0, 0, cols[g], :].astype(BF16), w[1][g])
        acc_ref[0, g] = acc
        carry_ref[0, g] = jnp.broadcast_to(carries[g], (8, lanes))
        _sample_write_out(o_ref, acc, own, g)


def _sb_sample_rest_kernel(q_ref, ckt_ref, cvt_ref, acc_in_ref, carry_in_ref, o_ref, acc_ref, carry_ref):
    past = ckt_ref.shape[3]
    lanes = SAMPLE_HEADS * q_ref.shape[1]
    n_grp = SAMPLE_GROUPS
    cols = [slice(g * SAMPLE_W, (g + 1) * SAMPLE_W) for g in range(n_grp)]
    upper = _upper_ones(ATT_TK)
    own, qm = _sample_queries(q_ref)
    acc_ref[...] = acc_in_ref[0]
    carry_ref[...] = carry_in_ref[0]

    def stick_left(carries):
        m = carries[0]
        for c in carries[1:]:
            m = jnp.maximum(m, c)
        return jnp.max(m) > STICK_FLOOR

    def more_keys(state):
        n, live = state
        return jnp.logical_and(n < past // ATT_TK - 1, live)

    def k_block(state):
        n, _ = state
        start = pl.multiple_of(past - (n + 2) * ATT_TK, ATT_TK)
        nz = [[_nt_dot(ckt_ref[0, 0, cols[g], pl.ds(start, ATT_TK)].T.astype(BF16), qm[g]) for g in range(n_grp)]]
        w, carries = _sb_weights(nz, [[None] * n_grp], [upper], [carry_ref[g, 0:1, :] for g in range(n_grp)])
        for g in range(n_grp):
            acc_ref[g] += _dot(cvt_ref[0, 0, cols[g], pl.ds(start, ATT_TK)].astype(BF16), w[0][g])
            carry_ref[g] = jnp.broadcast_to(carries[g], (8, lanes))
        return n + 1, stick_left(carries)

    lax.while_loop(more_keys, k_block,
                   (jnp.int32(0), stick_left([carry_in_ref[0, g, 0:1, :] for g in range(n_grp)])))
    for g in range(n_grp):
        _sample_write_out(o_ref, acc_ref[g], own, g)


def sb_attention_sample(q, kn, vn, cache_kt, cache_vt, slot):
    b, tq, d = q.shape
    past = cache_kt.shape[3]
    width = SAMPLE_GROUPS * SAMPLE_W
    assert past % ATT_TK == 0 and d % width == 0
    lanes = SAMPLE_HEADS * tq
    n_steps = d // width
    new_spec = pl.BlockSpec((1, tq, width), lambda i, j: (i, 0, j))
    last_spec = pl.BlockSpec((1, 1, width, ATT_TK), lambda i, j: (slot, i, j, past // ATT_TK - 1))
    cache_spec = pl.BlockSpec((1, 1, width, past), lambda i, j: (slot, i, j, 0))
    acc_spec = pl.BlockSpec((1, SAMPLE_GROUPS, SAMPLE_W, lanes), lambda i, j: (i, j, 0, 0))
    carry_spec = pl.BlockSpec((1, SAMPLE_GROUPS, 8, lanes), lambda i, j: (i, j, 0, 0))
    acc_shape = jax.ShapeDtypeStruct((b, n_steps * SAMPLE_GROUPS, SAMPLE_W, lanes), F32)
    carry_shape = jax.ShapeDtypeStruct((b, n_steps * SAMPLE_GROUPS, 8, lanes), F32)
    out, acc, carry = pl.pallas_call(
        _sb_sample_first_kernel,
        grid=(b, n_steps),
        in_specs=[new_spec, new_spec, new_spec, last_spec, last_spec],
        out_specs=[new_spec, acc_spec, carry_spec],
        out_shape=[jax.ShapeDtypeStruct(q.shape, BF16), acc_shape, carry_shape],
        compiler_params=_cparams(("parallel", "parallel")),
        name="sb_attn_sample_first",
    )(q, kn, vn, cache_kt, cache_vt)
    if past == ATT_TK:
        return out

    def rest():
        return pl.pallas_call(
            _sb_sample_rest_kernel,
            grid=(b, n_steps),
            in_specs=[new_spec, cache_spec, cache_spec, acc_spec, carry_spec],
            out_specs=new_spec,
            out_shape=jax.ShapeDtypeStruct(q.shape, BF16),
            scratch_shapes=[pltpu.VMEM((SAMPLE_GROUPS, SAMPLE_W, lanes), F32),
                            pltpu.VMEM((SAMPLE_GROUPS, 8, lanes), F32)],
            compiler_params=_cparams(("parallel", "parallel")),
            name="sb_attn_sample_rest",
        )(q, cache_kt, cache_vt, acc, carry)

    return lax.cond(jnp.max(carry) > STICK_FLOOR, rest, lambda: out)


TAIL_ROWS = 8
GATE_K = 512
LRU_SUB = 256
LRU_SUBS_PER_STEP = 2


def _gate_tile_starts():
    starts = []
    for n in range(D_RNN // V7X_MXU_COLS):
        first = (n * V7X_MXU_COLS) // LRU_BLOCK * LRU_BLOCK
        last = ((n + 1) * V7X_MXU_COLS - 1) // LRU_BLOCK * LRU_BLOCK + LRU_BLOCK
        lo = min(first // 128 * 128, D_RNN - GATE_K)
        assert lo <= first and last <= lo + GATE_K
        starts.append(lo)
    return starts


def _gate_slabs(w_a, w_x):
    def dense(w):
        eye = jnp.eye(LRU_BLOCKS, dtype=w.dtype)
        return jnp.einsum('njk,nm->njmk', w, eye).reshape(D_RNN, D_RNN)
    da, dx = dense(w_a), dense(w_x)
    slabs = []
    for n, lo in enumerate(_gate_tile_starts()):
        cols = slice(n * V7X_MXU_COLS, (n + 1) * V7X_MXU_COLS)
        slabs.append(jnp.concatenate([da[lo:lo + GATE_K, cols], dx[lo:lo + GATE_K, cols]], axis=1))
    return jnp.stack(slabs).astype(BF16)


def _gelu_tanh(x):
    c = math.sqrt(2.0 / math.pi)
    half_x = 0.5 * x
    return half_x + half_x * jnp.tanh(x * (c + (0.044715 * c) * (x * x)))


def _lru_kernel(x_ref, g_ref, win_ref, cw_ref, cb_ref, wg_ref, ba_ref, bx_ref, lam_ref, h0_ref, c0_ref,
                y_ref, hn_ref, cn_ref, ext_ref, a_ref, b_ref, h_ref, hstate_ref):
    tc = x_ref.shape[1]
    n_sub, sub = a_ref.shape[0], a_ref.shape[1]
    step = pl.program_id(1)

    @pl.when(step == 0)
    def _():
        hstate_ref[...] = h0_ref[0]
        ext_ref[0:TAIL_ROWS - (CONV_W - 1), :] = jnp.zeros((TAIL_ROWS - (CONV_W - 1), D_RNN), F32)
        ext_ref[TAIL_ROWS - (CONV_W - 1):TAIL_ROWS, :] = c0_ref[0]

    lam = lam_ref[...]
    softplus_neg_lam = jnp.maximum(-lam, 0.0) + jnp.log(1.0 + jnp.exp(-jnp.abs(lam)))
    decay = -LRU_C * softplus_neg_lam
    h = hstate_ref[...]
    for s in range(n_sub):
        r0 = s * sub
        xn = _rms(x_ref[0, r0:r0 + sub, :], g_ref[...]).astype(BF16)
        gx = _dot(xn, win_ref[...])
        gate = _gelu_tanh(gx[:, :D_RNN])
        u = gx[:, D_RNN:]
        ext_ref[TAIL_ROWS + r0:TAIL_ROWS + r0 + sub, :] = u
        c = cb_ref[...] + u * cw_ref[CONV_W - 1:CONV_W, :]
        for j in range(CONV_W - 1):
            first = TAIL_ROWS + r0 - (CONV_W - 1 - j)
            c = c + ext_ref[first:first + sub, :] * cw_ref[j:j + 1, :]
        cbf = c.astype(BF16)
        pre = [_dot(cbf[:, lo:lo + GATE_K], wg_ref[n]) for n, lo in enumerate(_gate_tile_starts())]
        r = _sigmoid(jnp.concatenate([p[:, :V7X_MXU_COLS] for p in pre], axis=-1) + ba_ref[...])
        gin = _sigmoid(jnp.concatenate([p[:, V7X_MXU_COLS:] for p in pre], axis=-1) + bx_ref[...])
        log_a = decay * r
        a = jnp.exp(log_a)
        a_ref[s] = a
        one_minus_a2 = -jnp.tanh(log_a) * (a * a + 1.0)
        root = jnp.where(one_minus_a2 > 0.0, one_minus_a2 * lax.rsqrt(one_minus_a2), 0.0)
        b_ref[s] = root * (gin * c)
        for i in range(sub):
            h = a_ref[s, i:i + 1, :] * h + b_ref[s, i:i + 1, :]
            h_ref[s, i:i + 1, :] = h
        y_ref[0, r0:r0 + sub, :] = (h_ref[s] * gate).astype(BF16)

    new_tail = ext_ref[tc + TAIL_ROWS - (CONV_W - 1):tc + TAIL_ROWS, :]
    ext_ref[TAIL_ROWS - (CONV_W - 1):TAIL_ROWS, :] = new_tail
    cn_ref[0] = new_tail
    hstate_ref[...] = h
    hn_ref[0] = h


def lru_core(x, g, w_in, conv_w, conv_b, w_gates, b_a, b_x, lam, h0, conv0):
    b, t, d = x.shape
    sub = min(LRU_SUB, t)
    tc = min(LRU_SUBS_PER_STEP * sub, t)
    assert t % tc == 0 and tc % sub == 0
    vec = _const_spec((1, D_RNN))
    return pl.pallas_call(
        _lru_kernel,
        grid=(b, t // tc),
        in_specs=[pl.BlockSpec((1, tc, d), lambda i, j: (i, j, 0)), _const_spec((1, D_MODEL)),
                  _const_spec(w_in.shape), _const_spec((CONV_W, D_RNN)), vec, _const_spec(w_gates.shape), vec,
                  vec, vec,
                  pl.BlockSpec((1, 1, D_RNN), lambda i, j: (i, 0, 0)),
                  pl.BlockSpec((1, CONV_W - 1, D_RNN), lambda i, j: (i, 0, 0))],
        out_specs=[pl.BlockSpec((1, tc, D_RNN), lambda i, j: (i, j, 0)),
                   pl.BlockSpec((1, 1, D_RNN), lambda i, j: (i, 0, 0)),
                   pl.BlockSpec((1, CONV_W - 1, D_RNN), lambda i, j: (i, 0, 0))],
        out_shape=[jax.ShapeDtypeStruct((b, t, D_RNN), BF16), jax.ShapeDtypeStruct((b, 1, D_RNN), F32),
                   jax.ShapeDtypeStruct((b, CONV_W - 1, D_RNN), F32)],
        scratch_shapes=[pltpu.VMEM((tc + TAIL_ROWS, D_RNN), F32)]
        + [pltpu.VMEM((tc // sub, sub, D_RNN), F32)] * 3 + [pltpu.VMEM((1, D_RNN), F32)],
        compiler_params=_cparams(("parallel", "arbitrary")),
        name="lru_core",
    )(x, g, w_in, conv_w, conv_b, w_gates, b_a, b_x, lam, h0, conv0)


def kernel(x_prompt, x_sample, p_prompt, p_sample, cache_sb_k, cache_sb_v, state_lru_h, state_lru_conv, norm_g,
           ffn_w_gate, ffn_w_up, ffn_w_down, ple_w_proj, ple_w_gate, sb_w_qkv, sb_w_o, lru_w_in, lru_conv_w,
           lru_conv_b, lru_w_a, lru_b_a, lru_w_x, lru_b_x, lru_lambda, lru_w_o, final_norm_g):
    depth = norm_g.shape[0]
    bp, tp, d = x_prompt.shape
    bs, ts, _ = x_sample.shape
    n_sb, _, past = cache_sb_k.shape[:3]
    xs = {'p': x_prompt.reshape(bp * tp, d), 's': x_sample.reshape(bs * ts, d)}
    ps = {'p': p_prompt.reshape(depth, bp * tp, PLE_DIM), 's': p_sample.reshape(depth, bs * ts, PLE_DIM)}
    dims = {'p': (bp, tp), 's': (bs, ts)}
    gf = final_norm_g.reshape(1, d)
    cache_kt = jnp.transpose(cache_sb_k, (0, 1, 3, 4, 2)).reshape(n_sb, bs, d, past)
    cache_vt = jnp.transpose(cache_sb_v, (0, 1, 3, 4, 2)).reshape(n_sb, bs, d, past)
    kv_prompt = None
    new_ks, new_vs = [], []
    new_h = {'p': [], 's': []}
    new_c = {'p': [], 's': []}

    wg_all, wu_all, wd_all = (w.astype(BF16) for w in (ffn_w_gate, ffn_w_up, ffn_w_down))
    wpe_all, wpg_all = ple_w_proj.astype(BF16), ple_w_gate.astype(BF16)

    for i in range(depth):
        j = i // 2
        g = norm_g[i].reshape(4, 1, d)
        if i % 2 == 0:
            wqkv = sb_w_qkv[j].astype(BF16)
            wq, wkt, wvt = wqkv[:, :d], wqkv[:, d:2 * d].T, wqkv[:, 2 * d:].T
            wo = sb_w_o[j].astype(BF16)
        else:
            w_in = lru_w_in[j].astype(BF16)
            w_gates = _gate_slabs(lru_w_a[j], lru_w_x[j])
            wo = lru_w_o[j].astype(BF16)
            vecs = [v[j].reshape(1, D_RNN) for v in (lru_conv_b, lru_b_a, lru_b_x, lru_lambda)]
        for grp in ('p', 's'):
            b, t = dims[grp]
            x = ffn1(xs[grp], g[0], wg_all, wu_all, wd_all, i)
            if i % 2 == 0 and grp == 'p':
                q, kt, vt, kb, vtb = qkv_proj_t(x.reshape(b, t, d), g[1], wq, wkt, wvt, j, n_sb, kv_prompt)
                kv_prompt = (kt, vt)
                a = sb_attention_prompt(q, kb, vtb).reshape(b * t, d)
            elif i % 2 == 0:
                q, k, v, kb, vb = qkv_proj(x, g[1], wqkv)
                new_ks.append(k.reshape(b, t, SB_HEADS, SB_HEAD_DIM))
                new_vs.append(v.reshape(b, t, SB_HEADS, SB_HEAD_DIM))
                q3, kb3, vb3 = (arr.reshape(b, t, d) for arr in (q, kb, vb))
                a = sb_attention_sample(q3, kb3, vb3, cache_kt, cache_vt, j).reshape(b * t, d)
            else:
                if grp == 'p':
                    h0 = jnp.zeros((b, 1, D_RNN), F32)
                    c0 = jnp.zeros((b, CONV_W - 1, D_RNN), F32)
                else:
                    h0 = state_lru_h[j].reshape(b, 1, D_RNN)
                    c0 = state_lru_conv[j]
                a, hn, cn = lru_core(x.reshape(b, t, d), g[1], w_in, lru_conv_w[j], vecs[0], w_gates, vecs[1],
                                     vecs[2], vecs[3], h0, c0)
                new_h[grp].append(hn.reshape(b, D_RNN))
                new_c[grp].append(cn)
                a = a.reshape(b * t, D_RNN)
            xs[grp] = ffn2_ple(x, a, wo, ps[grp], i, g[2], wg_all, wu_all, wd_all, g[3], wpe_all, wpg_all, gf,
                               final_norm=(i == depth - 1))

    def time_major(kvt):
        return jnp.transpose(kvt.reshape(n_sb, bp, SB_HEADS, SB_HEAD_DIM, tp), (0, 1, 4, 2, 3))

    return (xs['p'].reshape(bp, tp, d), xs['s'].reshape(bs, ts, d),
            time_major(kv_prompt[0]), time_major(kv_prompt[1]), jnp.stack(new_h['p']), jnp.stack(new_c['p']),
            jnp.stack(new_ks), jnp.stack(new_vs), jnp.stack(new_h['s']), jnp.stack(new_c['s']))
```
